```python
import jax, jax.numpy as jnp
from jax import lax
import numpy as np

D_MODEL = 1024
BATCH = 8
SEQ = 2048
DEPTH = 2
DEC_BATCH = 128
DEC_SEQ = 4
PAST_LEN = 2048
PAGE_SIZE = 128

HEAD_DIM = 64
N_HEADS_A = D_MODEL // HEAD_DIM
MOBA_BLOCK = 256
MOBA_TOPK = 3
MOBA_QCHUNK = 128
DIL_GROUPS = ((128, 1), (512, 4), (2048, 16))
N_GROUPS_B = len(DIL_GROUPS)
N_HEADS_B = D_MODEL // (2 * HEAD_DIM)
W_MAX = max(w for w, _ in DIL_GROUPS)
ROT_DIM = HEAD_DIM // 4
ROPE_THETA = 500000.0
ATTN_SCALE = HEAD_DIM ** -0.5
D_FF = 2816
N_EXPERTS = 8
TOP_K = 2
D_FF_EXPERT = 3584
MOE_BLOCK = 128
N_A_LAYERS = DEPTH // 2
N_B_LAYERS = DEPTH - N_A_LAYERS
N_DENSE = (DEPTH + 1) // 2
N_MOE = DEPTH // 2
RMS_EPS = 1e-6

kernel_name = 'yoco_moba_dilated_moe_step'


def rmsnorm(x, g):
    xf = x.astype(jnp.float32)
    y = xf * lax.rsqrt(jnp.mean(xf * xf, axis=-1, keepdims=True) + RMS_EPS)
    return (y * g.astype(jnp.float32)).astype(x.dtype)


def rope(x, pos):
    half = ROT_DIM // 2
    inv = ROPE_THETA ** (-2.0 * jnp.arange(half, dtype=jnp.float32) / ROT_DIM)
    ang = pos.astype(jnp.float32)[:, None] * inv[None, :]
    shp = (pos.shape[0],) + (1,) * (x.ndim - 3) + (half,)
    cos = jnp.cos(ang).reshape(shp)
    sin = jnp.sin(ang).reshape(shp)
    xf = x.astype(jnp.float32)
    x1 = xf[..., :half]
    x2 = xf[..., half:ROT_DIM]
    return jnp.concatenate([x1 * cos - x2 * sin, x2 * cos + x1 * sin, xf[..., ROT_DIM:]], axis=-1).astype(x.dtype)


def swiglu(x, w_gu, w_down):
    g, u = jnp.split(x @ w_gu, 2, axis=-1)
    return (jax.nn.silu(g) * u) @ w_down


def moe_ffn(x, w_router, w_gu, w_down):
    n, d = x.shape
    logits = jnp.einsum('nd,de->ne', x, w_router, preferred_element_type=jnp.float32)
    top_val, top_idx = lax.top_k(logits, TOP_K)
    gates = jax.nn.softmax(top_val, axis=-1).astype(x.dtype)
    a = n * TOP_K
    e_flat = top_idx.reshape(-1)
    tok_flat = jnp.arange(a, dtype=jnp.int32) // TOP_K
    order = jnp.argsort(e_flat)
    e_sorted = e_flat[order]
    counts = jnp.bincount(e_flat, length=N_EXPERTS)
    starts = jnp.cumsum(counts) - counts
    padded = (counts + MOE_BLOCK - 1) // MOE_BLOCK * MOE_BLOCK
    pends = jnp.cumsum(padded)
    pstarts = pends - padded
    dest = pstarts[e_sorted] + (jnp.arange(a) - starts[e_sorted])
    n_blocks = -(-a // MOE_BLOCK) + N_EXPERTS
    slot_tok = jnp.full((n_blocks * MOE_BLOCK,), n, jnp.int32).at[dest].set(tok_flat[order])
    block_exp = jnp.clip(jnp.searchsorted(pends, jnp.arange(n_blocks) * MOE_BLOCK, side='right'), 0, N_EXPERTS - 1)
    x_pad = jnp.concatenate([x, jnp.zeros((1, d), x.dtype)], axis=0)

    def run_block(args):
        toks, e = args
        return swiglu(x_pad[toks], w_gu[e], w_down[e])

    out = lax.map(run_block, (slot_tok.reshape(n_blocks, MOE_BLOCK), block_exp)).reshape(-1, d)
    dest_orig = jnp.zeros_like(dest).at[order].set(dest)
    return jnp.sum(out[dest_orig].reshape(n, TOP_K, d) * gates[..., None], axis=1)


def a_qkv(x, pos, norm_g, w_qkv, qn, kn):
    n, t, _ = x.shape
    qkv = (rmsnorm(x, norm_g) @ w_qkv).reshape(n, t, 3, N_HEADS_A, HEAD_DIM)
    q = rope(rmsnorm(qkv[:, :, 0], qn), pos)
    k = rope(rmsnorm(qkv[:, :, 1], kn), pos)
    return q, k, qkv[:, :, 2]


def moba_one(q, k_seq, v_seq, q_start):
    qc, h, dh = q.shape
    nblk = k_seq.shape[0] // MOBA_BLOCK
    kb = k_seq.reshape(nblk, MOBA_BLOCK, h, dh)
    vb = v_seq.reshape(nblk, MOBA_BLOCK, h, dh)
    k_mean = jnp.mean(kb.astype(jnp.float32), axis=1)
    own = q_start // MOBA_BLOCK
    gate = jnp.einsum('qhd,nhd->qhn', q.astype(jnp.float32), k_mean)
    gate = jnp.where((jnp.arange(nblk) < own)[None, None, :], gate, -jnp.inf)
    n_sel = min(MOBA_TOPK, nblk)
    _, sel = lax.top_k(gate, n_sel)
    sel_ok = sel < own
    hid = jnp.arange(h)[None, :, None]
    k_sel = kb.transpose(2, 0, 1, 3)[hid, sel]
    v_sel = vb.transpose(2, 0, 1, 3)[hid, sel]
    s_sel = jnp.einsum('qhd,qhjkd->qhjk', q, k_sel, preferred_element_type=jnp.float32) * ATTN_SCALE
    s_sel = jnp.where(sel_ok[..., None], s_sel, -jnp.inf).reshape(qc, h, n_sel * MOBA_BLOCK)
    k_own = lax.dynamic_slice_in_dim(k_seq, own * MOBA_BLOCK, MOBA_BLOCK, axis=0)
    v_own = lax.dynamic_slice_in_dim(v_seq, own * MOBA_BLOCK, MOBA_BLOCK, axis=0)
    s_own = jnp.einsum('qhd,khd->qhk', q, k_own, preferred_element_type=jnp.float32) * ATTN_SCALE
    kpos = own * MOBA_BLOCK + jnp.arange(MOBA_BLOCK)
    qpos = q_start + jnp.arange(qc)
    s_own = jnp.where((kpos[None, :] <= qpos[:, None])[:, None, :], s_own, -jnp.inf)
    p = jax.nn.softmax(jnp.concatenate([s_sel, s_own], axis=-1), axis=-1).astype(v_seq.dtype)
    n_k = n_sel * MOBA_BLOCK
    o = jnp.einsum('qhk,qhkd->qhd', p[..., :n_k], v_sel.reshape(qc, h, n_k, dh))
    return o + jnp.einsum('qhk,khd->qhd', p[..., n_k:], v_own)


def moba_prompt(q, k, v):
    b, s, h, dh = q.shape
    pad = (-s) % MOBA_BLOCK
    kp = jnp.pad(k, ((0, 0), (0, pad), (0, 0), (0, 0)))
    vp = jnp.pad(v, ((0, 0), (0, pad), (0, 0), (0, 0)))
    qc = min(MOBA_QCHUNK, s)
    nc = s // qc

    def one(i):
        bi = i // nc
        start = (i % nc) * qc
        q_c = lax.dynamic_slice_in_dim(q[bi], start, qc, axis=0)
        return moba_one(q_c, kp[bi], vp[bi], start)

    return lax.map(one, jnp.arange(b * nc)).reshape(b, s, h, dh)


def moba_sample(q, k_new, v_new, pool_k, pool_v, page_table):
    nb, ns, h, dh = q.shape
    past = page_table.shape[1] * pool_k.shape[1]
    pad = (-(past + ns)) % MOBA_BLOCK

    def one(args):
        q_b, kn, vn, pages = args
        kseq = jnp.concatenate([pool_k[pages].reshape(past, h, dh).astype(kn.dtype), kn, jnp.zeros((pad, h, dh), kn.dtype)], axis=0)
        vseq = jnp.concatenate([pool_v[pages].reshape(past, h, dh).astype(vn.dtype), vn, jnp.zeros((pad, h, dh), vn.dtype)], axis=0)
        return moba_one(q_b, kseq, vseq, past)

    return lax.map(one, (q, k_new, v_new, page_table))


def shared_kv(x, pos, norm_g, w_kv, kn):
    n, t, _ = x.shape
    kv = (rmsnorm(x, norm_g) @ w_kv).reshape(n, t, 2, N_HEADS_B, HEAD_DIM)
    return rope(rmsnorm(kv[:, :, 0], kn), pos), kv[:, :, 1]


def b_query(x, pos, norm_g, w_q, qn):
    n, t, _ = x.shape
    q = (rmsnorm(x, norm_g) @ w_q).reshape(n, t, N_GROUPS_B, N_HEADS_B, HEAD_DIM)
    return rope(rmsnorm(q, qn), pos)


def dilated_prompt(q, k, v, dilation, span):
    b, s, h, dh = q.shape
    l = s // dilation
    nb = -(-l // span)
    lp = nb * span

    def strided(x):
        return x.reshape(b, l, dilation, h, dh).transpose(0, 2, 1, 3, 4)

    qs = jnp.pad(strided(q), ((0, 0), (0, 0), (0, lp - l), (0, 0), (0, 0))).reshape(b, dilation, nb, span, h, dh)

    def band(x):
        xp = jnp.pad(strided(x), ((0, 0), (0, 0), (span, lp - l), (0, 0), (0, 0)))
        prev = xp[:, :, :lp].reshape(b, dilation, nb, span, h, dh)
        cur = xp[:, :, span:].reshape(b, dilation, nb, span, h, dh)
        return jnp.concatenate([prev, cur], axis=3)

    kb, vb = band(k), band(v)
    sc = jnp.einsum('bdnqhc,bdnkhc->bdnhqk', qs, kb, preferred_element_type=jnp.float32) * ATTN_SCALE
    qa = jnp.arange(span)[:, None]
    kbi = jnp.arange(2 * span)[None, :]
    dist = qa - kbi + span
    m_k = jnp.arange(nb)[:, None, None] * span + kbi[None] - span
    mask = ((dist >= 0) & (dist <= span))[None] & (m_k >= 0)
    sc = jnp.where(mask[None, None, :, None], sc, -jnp.inf)
    lse = jax.nn.logsumexp(sc, axis=-1)
    p = jnp.exp(sc - lse[..., None]).astype(v.dtype)
    o = jnp.einsum('bdnhqk,bdnkhc->bdnqhc', p, vb)
    o = o.reshape(b, dilation, lp, h, dh)[:, :, :l].transpose(0, 2, 1, 3, 4).reshape(b, s, h, dh)
    lse = lse.transpose(0, 1, 2, 4, 3).reshape(b, dilation, lp, h)[:, :, :l].transpose(0, 2, 1, 3).reshape(b, s, h)
    return o, lse


def dilated_sample(q, k_ctx, v_ctx, dilation, span):
    qn = q.shape[1]
    w_buf = k_ctx.shape[1] - qn
    idx = w_buf + jnp.arange(qn)[:, None] - jnp.arange(span + 1)[None, :] * dilation
    valid = idx >= 0
    idxc = jnp.maximum(idx, 0)
    ks = k_ctx[:, idxc]
    vs = v_ctx[:, idxc]
    sc = jnp.einsum('nqhc,nqjhc->nqhj', q, ks, preferred_element_type=jnp.float32) * ATTN_SCALE
    sc = jnp.where(valid[None, :, None, :], sc, -jnp.inf)
    lse = jax.nn.logsumexp(sc, axis=-1)
    p = jnp.exp(sc - lse[..., None]).astype(v_ctx.dtype)
    return jnp.einsum('nqhj,nqjhc->nqhc', p, vs), lse


def merge_groups(outs, lses):
    w = jax.nn.softmax(jnp.stack(lses, axis=0), axis=0)
    o = jnp.sum(w[..., None] * jnp.stack(outs, axis=0).astype(jnp.float32), axis=0)
    return o.astype(outs[0].dtype)


def setup_inputs(seed: int = 0) -> dict:
    key = jax.random.key(seed)
    ks = jax.random.split(key, 32)
    n_pages = PAST_LEN // PAGE_SIZE
    n_used = DEC_BATCH * n_pages
    n_phys = n_used + n_used // 4
    w_buf = min(W_MAX, PAST_LEN)

    def nrm(k, shape, scale=1.0):
        return jax.random.normal(k, shape, jnp.float32) * scale

    def gain(k, shape):
        return 1.0 + 0.1 * jax.random.normal(k, shape, jnp.float32)

    return {
        'x_prompt': nrm(ks[0], (BATCH, SEQ, D_MODEL)),
        'x_sample': nrm(ks[1], (DEC_BATCH, DEC_SEQ, D_MODEL)),
        'cache_a_k': nrm(ks[2], (N_A_LAYERS, n_phys, PAGE_SIZE, N_HEADS_A, HEAD_DIM)),
        'cache_a_v': nrm(ks[3], (N_A_LAYERS, n_phys, PAGE_SIZE, N_HEADS_A, HEAD_DIM)),
        'cache_b_k': nrm(ks[4], (DEC_BATCH, w_buf, N_HEADS_B, HEAD_DIM)),
        'cache_b_v': nrm(ks[5], (DEC_BATCH, w_buf, N_HEADS_B, HEAD_DIM)),
        'page_table': jax.random.permutation(ks[6], n_phys)[:n_used].reshape(DEC_BATCH, n_pages).astype(jnp.int32),
        'norm_attn_a': gain(ks[7], (N_A_LAYERS, D_MODEL)),
        'w_qkv_a': nrm(ks[8], (N_A_LAYERS, D_MODEL, 3 * N_HEADS_A * HEAD_DIM), D_MODEL ** -0.5),
        'qnorm_a': gain(ks[9], (N_A_LAYERS, HEAD_DIM)),
        'knorm_a': gain(ks[10], (N_A_LAYERS, HEAD_DIM)),
        'w_o_a': nrm(ks[11], (N_A_LAYERS, N_HEADS_A * HEAD_DIM, D_MODEL), (N_HEADS_A * HEAD_DIM) ** -0.5),
        'norm_kv_b': gain(ks[12], (D_MODEL,)),
        'w_kv_b': nrm(ks[13], (D_MODEL, 2 * N_HEADS_B * HEAD_DIM), D_MODEL ** -0.5),
        'knorm_b': gain(ks[14], (HEAD_DIM,)),
        'norm_attn_b': gain(ks[15], (N_B_LAYERS, D_MODEL)),
        'w_q_b': nrm(ks[16], (N_B_LAYERS, D_MODEL, N_GROUPS_B * N_HEADS_B * HEAD_DIM), D_MODEL ** -0.5),
        'qnorm_b': gain(ks[17], (N_B_LAYERS, HEAD_DIM)),
        'w_o_b': nrm(ks[18], (N_B_LAYERS, N_HEADS_B * HEAD_DIM, D_MODEL), (N_HEADS_B * HEAD_DIM) ** -0.5),
        'norm_ffn': gain(ks[19], (DEPTH, D_MODEL)),
        'w_gu_dense': nrm(ks[20], (N_DENSE, D_MODEL, 2 * D_FF), D_MODEL ** -0.5),
        'w_down_dense': nrm(ks[21], (N_DENSE, D_FF, D_MODEL), D_FF ** -0.5),
        'w_router': nrm(ks[22], (N_MOE, D_MODEL, N_EXPERTS), D_MODEL ** -0.5),
        'w_gu_moe': nrm(ks[23], (N_MOE, N_EXPERTS, D_MODEL, 2 * D_FF_EXPERT), D_MODEL ** -0.5),
        'w_down_moe': nrm(ks[24], (N_MOE, N_EXPERTS, D_FF_EXPERT, D_MODEL), D_FF_EXPERT ** -0.5),
    }


def reference(x_prompt, x_sample, cache_a_k, cache_a_v, cache_b_k, cache_b_v, page_table,
              norm_attn_a, w_qkv_a, qnorm_a, knorm_a, w_o_a,
              norm_kv_b, w_kv_b, knorm_b, norm_attn_b, w_q_b, qnorm_b, w_o_b,
              norm_ffn, w_gu_dense, w_down_dense, w_router, w_gu_moe, w_down_moe):
    b, s, d = x_prompt.shape
    nb, ns, _ = x_sample.shape
    past = page_table.shape[1] * cache_a_k.shape[2]
    pos_p = jnp.arange(s)
    pos_s = past + jnp.arange(ns)
    xp, xs = x_prompt, x_sample
    ak_p, av_p, ak_s, av_s = [], [], [], []
    for l in range(DEPTH):
        if l < N_A_LAYERS:
            qp, kp, vp = a_qkv(xp, pos_p, norm_attn_a[l], w_qkv_a[l], qnorm_a[l], knorm_a[l])
            qs, ks_, vs_ = a_qkv(xs, pos_s, norm_attn_a[l], w_qkv_a[l], qnorm_a[l], knorm_a[l])
            op = moba_prompt(qp, kp, vp)
            os_ = moba_sample(qs, ks_, vs_, cache_a_k[l], cache_a_v[l], page_table)
            xp = xp + op.reshape(b, s, -1) @ w_o_a[l]
            xs = xs + os_.reshape(nb, ns, -1) @ w_o_a[l]
            ak_p.append(kp)
            av_p.append(vp)
            ak_s.append(ks_)
            av_s.append(vs_)
        else:
            j = l - N_A_LAYERS
            qp = b_query(xp, pos_p, norm_attn_b[j], w_q_b[j], qnorm_b[j])
            qs = b_query(xs, pos_s, norm_attn_b[j], w_q_b[j], qnorm_b[j])
            outs_p, lses_p, outs_s, lses_s = [], [], [], []
            for g, (win, dil) in enumerate(DIL_GROUPS):
                o_g, l_g = dilated_prompt(qp[:, :, g], kb_p, vb_p, dil, win // dil)
                outs_p.append(o_g)
                lses_p.append(l_g)
                o_g, l_g = dilated_sample(qs[:, :, g], k_ctx, v_ctx, dil, win // dil)
                outs_s.append(o_g)
                lses_s.append(l_g)
            xp = xp + merge_groups(outs_p, lses_p).reshape(b, s, -1) @ w_o_b[j]
            xs = xs + merge_groups(outs_s, lses_s).reshape(nb, ns, -1) @ w_o_b[j]
        tok = jnp.concatenate([xp.reshape(-1, d), xs.reshape(-1, d)], axis=0)
        h = rmsnorm(tok, norm_ffn[l])
        if l % 2 == 0:
            f = swiglu(h, w_gu_dense[l // 2], w_down_dense[l // 2])
        else:
            f = moe_ffn(h, w_router[l // 2], w_gu_moe[l // 2], w_down_moe[l // 2])
        tok = tok + f
        xp = tok[:b * s].reshape(b, s, d)
        xs = tok[b * s:].reshape(nb, ns, d)
        if l == N_A_LAYERS - 1:
            kb_p, vb_p = shared_kv(xp, pos_p, norm_kv_b, w_kv_b, knorm_b)
            kb_s, vb_s = shared_kv(xs, pos_s, norm_kv_b, w_kv_b, knorm_b)
            k_ctx = jnp.concatenate([cache_b_k.astype(kb_s.dtype), kb_s], axis=1)
            v_ctx = jnp.concatenate([cache_b_v.astype(vb_s.dtype), vb_s], axis=1)
    keep = min(W_MAX, s)
    a_k_prompt = jnp.stack(ak_p, axis=0)
    a_v_prompt = jnp.stack(av_p, axis=0)
    a_k_sample = jnp.stack(ak_s, axis=0)
    a_v_sample = jnp.stack(av_s, axis=0)
    b_k_prompt = kb_p[:, s - keep:]
    b_v_prompt = vb_p[:, s - keep:]
    return (xp, xs, a_k_prompt, a_v_prompt, b_k_prompt, b_v_prompt, a_k_sample, a_v_sample, kb_s, vb_s)
```

```python
import functools

import jax
import jax.numpy as jnp
from jax import lax
from jax.experimental import pallas as pl
from jax.experimental.pallas import tpu as pltpu

F32 = jnp.float32
BF16 = jnp.bfloat16
I32 = jnp.int32
HIGHEST = lax.Precision.HIGHEST
NEG_INF = float("-inf")

HEAD_DIM = 64
MOBA_BLOCK = 256
MOBA_TOPK = 3
DIL_GROUPS = ((128, 1), (512, 4), (2048, 16))
DIL_SPAN = 128
ROT_DIM = HEAD_DIM // 4
ROPE_THETA = 500000.0
ATTN_SCALE = HEAD_DIM ** -0.5
RMS_EPS = 1e-6
N_EXPERTS = 8
TOP_K = 2

LANES = 128
HEADS_PER_VREG = LANES // HEAD_DIM
VMEM_LIMIT_BYTES = 56 * 1024 * 1024

TM_TOKENS = 512
TN_PROJ = 512
TM_FFN = 1024
TF_FFN = 256
TM_MOE = 512
TN_MOE_GU = 896
TN_MOE_DOWN = 512
TM_ROUTE = 512


def _cparams(*sem):
    return pltpu.CompilerParams(dimension_semantics=sem, vmem_limit_bytes=VMEM_LIMIT_BYTES)


def _rmsnorm_rows(x, g):
    ms = jnp.mean(x * x, axis=-1, keepdims=True)
    return x * lax.rsqrt(ms + RMS_EPS) * g


def _norm_proj_kernel(x_ref, g_ref, w_ref, hg_ref, c_ref, s1_ref, s2_ref, bd_ref, *rest, parts, tn):
    out_refs, h_ref = rest[:len(parts)], rest[len(parts)]
    j = pl.program_id(1)

    @pl.when(j == 0)
    def _():
        h_ref[...] = _rmsnorm_rows(x_ref[...], g_ref[...]).astype(BF16)

    y = jnp.dot(h_ref[...], w_ref[...].astype(BF16), preferred_element_type=F32)

    def head_norm_rope(o_ref):
        c, s1, s2 = c_ref[...], s1_ref[...], s2_ref[...]
        bd = bd_ref[...]
        for cb in range(tn // LANES):
            cols = slice(cb * LANES, (cb + 1) * LANES)
            yc = y[:, cols]
            y2 = yc * yc
            hi = y2.astype(BF16)
            lo = (y2 - hi.astype(F32)).astype(BF16)
            ss = jnp.dot(jnp.concatenate([hi, lo], axis=1), bd, preferred_element_type=F32)
            yn = yc * lax.rsqrt(ss * (1.0 / HEAD_DIM) + RMS_EPS) * hg_ref[:, cols]
            up = pltpu.roll(yn, LANES - ROT_DIM // 2, 1)
            dn = pltpu.roll(yn, ROT_DIM // 2, 1)
            o_ref[:, cols] = yn * c + up * s1 + dn * s2

    for o_ref, (start, ntiles, normed) in zip(out_refs, parts):
        @pl.when((j >= start) & (j < start + ntiles))
        def _(o_ref=o_ref, normed=normed):
            if normed:
                head_norm_rope(o_ref)
            else:
                o_ref[...] = y


def _norm_proj(x, g, w, head_gain, rope_tabs, parts):
    n, d = x.shape
    m = w.shape[1]
    tm, tn = min(TM_TOKENS, n), TN_PROJ
    assert n % tm == 0 and m % tn == 0 and all(p[0] % tn == 0 for p in parts)
    tile_parts, start = [], 0
    for width, normed in parts:
        tile_parts.append((start, width // tn, normed))
        start += width // tn
    c, s1, s2 = rope_tabs
    ones_bd = (jnp.arange(LANES)[:, None] // HEAD_DIM == jnp.arange(LANES)[None, :] // HEAD_DIM)
    bd = jnp.concatenate([ones_bd, ones_bd], axis=0).astype(BF16)

    def out_map(start, ntiles):
        return lambda i, j: (i, jnp.clip(j - start, 0, ntiles - 1))

    tab_spec = pl.BlockSpec((tm, LANES), lambda i, j: (i, 0))
    return pl.pallas_call(
        functools.partial(_norm_proj_kernel, parts=tuple(tile_parts), tn=tn),
        grid=(n // tm, m // tn),
        in_specs=[
            pl.BlockSpec((tm, d), lambda i, j: (i, 0)),
            pl.BlockSpec((1, d), lambda i, j: (0, 0)),
            pl.BlockSpec((d, tn), lambda i, j: (0, j)),
            pl.BlockSpec((1, tn), lambda i, j: (0, j)),
            tab_spec, tab_spec, tab_spec,
            pl.BlockSpec((2 * LANES, LANES), lambda i, j: (0, 0)),
        ],
        out_specs=[pl.BlockSpec((tm, tn), out_map(st, nt)) for st, nt, _ in tile_parts],
        out_shape=[jax.ShapeDtypeStruct((n, width), F32) for width, _ in parts],
        scratch_shapes=[pltpu.VMEM((tm, d), BF16)],
        compiler_params=_cparams("parallel", "arbitrary"),
        name="norm_proj",
    )(x, g.reshape(1, d), w, head_gain.reshape(1, m), c, s1, s2, bd)


def _rope_tables(pos):
    half = ROT_DIM // 2
    inv = ROPE_THETA ** (-2.0 * jnp.arange(half, dtype=F32) / ROT_DIM)
    ang = pos.astype(F32)[:, None] * inv[None, :]
    cos, sin = jnp.cos(ang), jnp.sin(ang)
    n = pos.shape[0]
    zeros = jnp.zeros((n, HEAD_DIM - ROT_DIM), F32)
    zh = jnp.zeros((n, half), F32)
    c = jnp.concatenate([cos, cos, jnp.ones((n, HEAD_DIM - ROT_DIM), F32)], axis=1)
    s1 = jnp.concatenate([-sin, zh, zeros], axis=1)
    s2 = jnp.concatenate([zh, sin, zeros], axis=1)
    tile = lambda t: jnp.tile(t, (1, HEADS_PER_VREG))
    return tile(c), tile(s1), tile(s2)


def _topk_block_mask(gate, n_valid, lane_idx, nblk):
    rank = jnp.zeros(gate.shape, I32)
    for m in range(nblk):
        gm = gate[:, m:m + 1]
        beats = (gm > gate) | ((gm == gate) & (m < lane_idx))
        rank = rank + jnp.where(beats & (m < n_valid), 1, 0)
    return jnp.where((lane_idx < n_valid) & (rank < MOBA_TOPK), 1.0, 0.0)


def _moba_prompt_kernel(q_ref, k_ref, v_ref, o_ref, kb_ref, vb_ref, km_ref, *, nblk):
    qi = pl.program_id(2)
    blk = MOBA_BLOCK

    @pl.when(qi == 0)
    def _():
        for n in range(nblk):
            rows = slice(n * blk, (n + 1) * blk)
            kn = k_ref[rows, :]
            kb_ref[rows, :] = kn.astype(BF16)
            vb_ref[rows, :] = v_ref[rows, :].astype(BF16)
            km_ref[n:n + 1, :] = jnp.mean(kn, axis=0, keepdims=True)

    q = q_ref[...]
    lane = lax.broadcasted_iota(I32, (blk, LANES), 1)
    row = lax.broadcasted_iota(I32, (blk, blk), 0)
    col = lax.broadcasted_iota(I32, (blk, blk), 1)
    blk_lane = lax.broadcasted_iota(I32, (blk, nblk), 1)
    km = km_ref[...]
    own0 = pl.multiple_of(qi * blk, blk)
    k_own = kb_ref[pl.ds(own0, blk), :]
    v_own = vb_ref[pl.ds(own0, blk), :]
    contract_last = (((1,), (1,)), ((), ()))

    outs = []
    for h in range(HEADS_PER_VREG):
        head_lanes = (lane // HEAD_DIM) == h
        qh = jnp.where(head_lanes, q, 0.0)
        gate = lax.dot_general(qh, km, contract_last, precision=HIGHEST, preferred_element_type=F32)
        sel = _topk_block_mask(gate, qi, blk_lane, nblk)
        qs = (qh * ATTN_SCALE).astype(BF16)

        s = lax.dot_general(qs, k_own, contract_last, preferred_element_type=F32)
        s = jnp.where(col <= row, s, NEG_INF)
        m0 = jnp.max(s, axis=1, keepdims=True)
        p = jnp.exp(s - m0)
        l0 = jnp.sum(p, axis=1, keepdims=True)
        a0 = jnp.dot(p.astype(BF16), v_own, preferred_element_type=F32)

        def past_block(j, carry, qs=qs, sel=sel):
            m, l, a = carry
            j0 = pl.multiple_of(j * blk, blk)
            s = lax.dot_general(qs, kb_ref[pl.ds(j0, blk), :], contract_last, preferred_element_type=F32)
            sel_j = jnp.max(jnp.where(blk_lane == j, sel, 0.0), axis=1, keepdims=True)
            s = jnp.where(sel_j > 0.0, s, NEG_INF)
            m_new = jnp.maximum(m, jnp.max(s, axis=1, keepdims=True))
            alpha = jnp.exp(m - m_new)
            p = jnp.exp(s - m_new)
            l = alpha * l + jnp.sum(p, axis=1, keepdims=True)
            a = alpha * a + jnp.dot(p.astype(BF16), vb_ref[pl.ds(j0, blk), :], preferred_element_type=F32)
            return m_new, l, a

        _, l, a = lax.fori_loop(0, qi, past_block, (m0, l0, a0))
        outs.append(a / l)
    o_ref[...] = jnp.where((lane // HEAD_DIM) == 0, outs[0], outs[1])


def _moba_prompt(q, k, v):
    b, s, hd = q.shape
    nblk = s // MOBA_BLOCK
    assert s % MOBA_BLOCK == 0 and hd % LANES == 0
    q_spec = pl.BlockSpec((None, MOBA_BLOCK, LANES), lambda bi, hp, qi: (bi, qi, hp))
    kv_spec = pl.BlockSpec((None, s, LANES), lambda bi, hp, qi: (bi, 0, hp))
    return pl.pallas_call(
        functools.partial(_moba_prompt_kernel, nblk=nblk),
        grid=(b, hd // LANES, nblk),
        in_specs=[q_spec, kv_spec, kv_spec],
        out_specs=q_spec,
        out_shape=jax.ShapeDtypeStruct((b, s, hd), F32),
        scratch_shapes=[pltpu.VMEM((s, LANES), BF16), pltpu.VMEM((s, LANES), BF16),
                        pltpu.VMEM((nblk, LANES), F32)],
        compiler_params=_cparams("parallel", "parallel", "arbitrary"),
        name="moba_prompt",
    )(q, k, v)


def _moba_sample_kernel(pt_ref, q_ref, kn_ref, vn_ref, ka_ref, kbp_ref, va_ref, vbp_ref, o_ref,
                        qf_ref, qb_ref, kb_ref, vb_ref, g_ref, m_ref, l_ref, acc_ref, *, nblk, ns, nh):
    del pt_ref
    n = pl.program_id(1)
    rows = ns * nh
    d = nh * HEAD_DIM
    page = MOBA_BLOCK // 2
    contract_last = (((1,), (1,)), ((), ()))

    @pl.when(n == 0)
    def _():
        q = q_ref[...]
        head_of_lane = lax.broadcasted_iota(I32, (nh, d), 1) // HEAD_DIM
        head_of_row = lax.broadcasted_iota(I32, (nh, d), 0)
        for qi in range(ns):
            piece = jnp.where(head_of_lane == head_of_row, jnp.broadcast_to(q[qi:qi + 1, :], (nh, d)), 0.0)
            qf_ref[qi * nh:(qi + 1) * nh, :] = piece
            qb_ref[qi * nh:(qi + 1) * nh, :] = (piece * ATTN_SCALE).astype(BF16)

    ka, kb2 = ka_ref[...], kbp_ref[...]
    kb_ref[0:page, :] = ka.astype(BF16)
    kb_ref[page:2 * page, :] = kb2.astype(BF16)
    vb_ref[0:page, :] = va_ref[...].astype(BF16)
    vb_ref[page:2 * page, :] = vbp_ref[...].astype(BF16)
    k_mean = (jnp.sum(ka, axis=0, keepdims=True) + jnp.sum(kb2, axis=0, keepdims=True)) * (1.0 / MOBA_BLOCK)
    gate = jnp.sum(qf_ref[...] * k_mean, axis=1, keepdims=True)
    s = lax.dot_general(qb_ref[...], kb_ref[...], contract_last, preferred_element_type=F32)
    m = jnp.max(s, axis=1, keepdims=True)
    p = jnp.exp(s - m)
    l = jnp.sum(p, axis=1, keepdims=True)
    g_ref[n] = jnp.broadcast_to(gate, (rows, LANES))
    m_ref[n] = jnp.broadcast_to(m, (rows, LANES))
    l_ref[n] = jnp.broadcast_to(l, (rows, LANES))
    acc_ref[n] = jnp.dot(p.astype(BF16), vb_ref[...], preferred_element_type=F32)

    @pl.when(n == nblk - 1)
    def _():
        sel = []
        for a in range(nblk):
            ga = g_ref[a]
            rank = jnp.zeros((rows, LANES), I32)
            for b in range(nblk):
                gb = g_ref[b]
                beats = (gb > ga) if b > a else (gb >= ga)
                if b != a:
                    rank = rank + jnp.where(beats, 1, 0)
            sel.append(rank < MOBA_TOPK)
        qf = qf_ref[...]
        kn, vn = kn_ref[...], vn_ref[...]
        q_of_row = lax.broadcasted_iota(I32, (rows, 1), 0) // nh
        s_own = [jnp.sum(qf * kn[j:j + 1, :], axis=1, keepdims=True) * ATTN_SCALE for j in range(ns)]
        s_own = [jnp.where(j <= q_of_row, s_own[j], NEG_INF) for j in range(ns)]
        m_blk = [jnp.where(sel[a], m_ref[a], NEG_INF)[:, 0:1] for a in range(nblk)]
        m_all = functools.reduce(jnp.maximum, s_own + m_blk)
        den = jnp.zeros((rows, 1), F32)
        num = jnp.zeros((rows, d), F32)
        for j in range(ns):
            w = jnp.exp(s_own[j] - m_all)
            den = den + w
            num = num + w * vn[j:j + 1, :]
        for a in range(nblk):
            w = jnp.exp(m_blk[a] - m_all)
            den = den + w * l_ref[a][:, 0:1]
            num = num + w * acc_ref[a]
        res = num / den
        head_of_lane = lax.broadcasted_iota(I32, (rows, d), 1) // HEAD_DIM
        head_of_row = lax.broadcasted_iota(I32, (rows, d), 0) % nh
        res = jnp.where(head_of_lane == head_of_row, res, 0.0)
        o_ref[...] = jnp.sum(res.reshape(ns, nh, d), axis=1)


def _moba_sample(q, k_new, v_new, pool_k, pool_v, page_table):
    nb, ns, d = q.shape
    page = pool_k.shape[1]
    nblk = page_table.shape[1] * page // MOBA_BLOCK
    nh = d // HEAD_DIM
    assert 2 * page == MOBA_BLOCK and page_table.shape[1] * page == nblk * MOBA_BLOCK
    assert ns <= MOBA_BLOCK and nh % 8 == 0
    rows = ns * nh
    new_spec = pl.BlockSpec((None, ns, d), lambda b, n, pt: (b, 0, 0))
    page_a = pl.BlockSpec((None, page, d), lambda b, n, pt: (pt[b, 2 * n], 0, 0))
    page_b = pl.BlockSpec((None, page, d), lambda b, n, pt: (pt[b, 2 * n + 1], 0, 0))
    return pl.pallas_call(
        functools.partial(_moba_sample_kernel, nblk=nblk, ns=ns, nh=nh),
        grid_spec=pltpu.PrefetchScalarGridSpec(
            num_scalar_prefetch=1,
            grid=(nb, nblk),
            in_specs=[new_spec, new_spec, new_spec, page_a, page_b, page_a, page_b],
            out_specs=new_spec,
            scratch_shapes=[
                pltpu.VMEM((rows, d), F32), pltpu.VMEM((rows, d), BF16),
                pltpu.VMEM((MOBA_BLOCK, d), BF16), pltpu.VMEM((MOBA_BLOCK, d), BF16),
                pltpu.VMEM((nblk, rows, LANES), F32), pltpu.VMEM((nblk, rows, LANES), F32),
                pltpu.VMEM((nblk, rows, LANES), F32), pltpu.VMEM((nblk, rows, d), F32),
            ]),
        out_shape=jax.ShapeDtypeStruct((nb, ns, d), F32),
        compiler_params=_cparams("parallel", "arbitrary"),
        name="moba_sample",
    )(page_table, q, k_new, v_new, pool_k, pool_k, pool_v, pool_v)


def _matmul_residual_kernel(a_ref, w_ref, r_ref, o_ref):
    o_ref[...] = r_ref[...] + jnp.dot(a_ref[...].astype(BF16), w_ref[...].astype(BF16),
                                      preferred_element_type=F32)


def _matmul_residual(a, w, res):
    n, k = a.shape
    m = w.shape[1]
    tm = min(TM_TOKENS, n)
    assert n % tm == 0
    return pl.pallas_call(
        _matmul_residual_kernel,
        grid=(n // tm,),
        in_specs=[pl.BlockSpec((tm, k), lambda i: (i, 0)),
                  pl.BlockSpec((k, m), lambda i: (0, 0)),
                  pl.BlockSpec((tm, m), lambda i: (i, 0))],
        out_specs=pl.BlockSpec((tm, m), lambda i: (i, 0)),
        out_shape=jax.ShapeDtypeStruct((n, m), F32),
        compiler_params=_cparams("parallel"),
        name="matmul_residual",
    )(a, w, res)


def _dense_ffn_kernel(x_ref, g_ref, wg_ref, wu_ref, wd_ref, o_ref, h_ref):
    @pl.when(pl.program_id(1) == 0)
    def _():
        x = x_ref[...]
        h_ref[...] = _rmsnorm_rows(x, g_ref[...]).astype(BF16)
        o_ref[...] = x
    h = h_ref[...]
    gate = jnp.dot(h, wg_ref[...].astype(BF16), preferred_element_type=F32)
    up = jnp.dot(h, wu_ref[...].astype(BF16), preferred_element_type=F32)
    act = (gate * jax.nn.sigmoid(gate)) * up
    o_ref[...] += jnp.dot(act.astype(BF16), wd_ref[...].astype(BF16), preferred_element_type=F32)


def _dense_ffn(x, g, w_gu, w_down):
    n, d = x.shape
    f = w_down.shape[0]
    tm, tf = min(TM_FFN, n), TF_FFN
    assert n % tm == 0 and f % tf == 0
    nf = f // tf
    return pl.pallas_call(
        _dense_ffn_kernel,
        grid=(n // tm, nf),
        in_specs=[pl.BlockSpec((tm, d), lambda i, c: (i, 0)),
                  pl.BlockSpec((1, d), lambda i, c: (0, 0)),
                  pl.BlockSpec((d, tf), lambda i, c: (0, c)),
                  pl.BlockSpec((d, tf), lambda i, c: (0, nf + c)),
                  pl.BlockSpec((tf, d), lambda i, c: (c, 0))],
        out_specs=pl.BlockSpec((tm, d), lambda i, c: (i, 0)),
        out_shape=jax.ShapeDtypeStruct((n, d), F32),
        scratch_shapes=[pltpu.VMEM((tm, d), BF16)],
        compiler_params=_cparams("parallel", "arbitrary"),
        name="dense_ffn",
    )(x, g.reshape(1, d), w_gu, w_gu, w_down)


def _merge_groups(outs, lses):
    m = functools.reduce(jnp.maximum, lses)
    ws = [jnp.exp(l - m) for l in lses]
    num = functools.reduce(lambda a, b: a + b, [w * o for w, o in zip(ws, outs)])
    return num / functools.reduce(lambda a, b: a + b, ws)


def _dilated_prompt_kernel(q0_ref, q1_ref, q2_ref, k_ref, v_ref, o_ref,
                           qd_ref, kd_ref, vd_ref, od_ref, ld_ref, og_ref, lg_ref, *, s):
    span = DIL_SPAN
    ntiles = s // span
    contract_last = (((1,), (1,)), ((), ()))
    lane = lax.broadcasted_iota(I32, (span, LANES), 1)
    row = lax.broadcasted_iota(I32, (span, span), 0)
    col = lax.broadcasted_iota(I32, (span, span), 1)

    for g, (q_ref, (_, dil)) in enumerate(zip((q0_ref, q1_ref, q2_ref), DIL_GROUPS)):
        sub = s // dil
        for r in range(dil):
            src = pl.ds(r, sub, stride=dil) if dil > 1 else pl.ds(0, s)
            dst = slice(r * sub, (r + 1) * sub)
            qd_ref[dst, :] = q_ref[src, :] * ATTN_SCALE
            kd_ref[dst, :] = k_ref[src, :].astype(BF16)
            vd_ref[dst, :] = v_ref[src, :].astype(BF16)
        tiles_per_class = sub // span

        def tile(t, carry, tiles_per_class=tiles_per_class):
            cur = pl.ds(pl.multiple_of(t * span, span), span)
            prev = pl.ds(pl.multiple_of(jnp.maximum(t - 1, 0) * span, span), span)
            prev_off = jnp.where((t % tiles_per_class) != 0, 0, span)
            qt = qd_ref[cur, :]
            kc, vc = kd_ref[cur, :], vd_ref[cur, :]
            kp, vp = kd_ref[prev, :], vd_ref[prev, :]
            vcat = jnp.concatenate([vp, vc], axis=0)
            outs, lses = [], []
            for h in range(HEADS_PER_VREG):
                qh = jnp.where((lane // HEAD_DIM) == h, qt, 0.0).astype(BF16)
                sp = lax.dot_general(qh, kp, contract_last, preferred_element_type=F32)
                sp = jnp.where(col >= row + prev_off, sp, NEG_INF)
                sc = lax.dot_general(qh, kc, contract_last, preferred_element_type=F32)
                sc = jnp.where(col <= row, sc, NEG_INF)
                sa = jnp.concatenate([sp, sc], axis=1)
                m = jnp.max(sa, axis=1, keepdims=True)
                p = jnp.exp(sa - m)
                l = jnp.sum(p, axis=1, keepdims=True)
                outs.append(jnp.dot(p.astype(BF16), vcat, preferred_element_type=F32) / l)
                lses.append(jnp.broadcast_to(m + jnp.log(l), (span, LANES)))
            first = (lane // HEAD_DIM) == 0
            od_ref[cur, :] = jnp.where(first, outs[0], outs[1])
            ld_ref[cur, :] = jnp.where(first, lses[0], lses[1])
            return carry

        lax.fori_loop(0, ntiles, tile, 0)
        for r in range(dil):
            dst = pl.ds(r, sub, stride=dil) if dil > 1 else pl.ds(0, s)
            src = slice(r * sub, (r + 1) * sub)
            og_ref[g, dst, :] = od_ref[src, :]
            lg_ref[g, dst, :] = ld_ref[src, :]

    ng = len(DIL_GROUPS)
    o_ref[...] = _merge_groups([og_ref[g] for g in range(ng)], [lg_ref[g] for g in range(ng)])


def _dilated_prompt(q, k, v):
    b, s, hd = k.shape
    ng = len(DIL_GROUPS)
    assert q.shape == (b, s, ng * hd) and hd % LANES == 0
    assert all(s % (DIL_SPAN * dil) == 0 and win // dil == DIL_SPAN for win, dil in DIL_GROUPS)
    hp_blocks = hd // LANES

    def q_spec(g):
        return pl.BlockSpec((None, s, LANES), lambda bi, hp: (bi, 0, g * hp_blocks + hp))

    kv_spec = pl.BlockSpec((None, s, LANES), lambda bi, hp: (bi, 0, hp))
    return pl.pallas_call(
        functools.partial(_dilated_prompt_kernel, s=s),
        grid=(b, hp_blocks),
        in_specs=[q_spec(0), q_spec(1), q_spec(2), kv_spec, kv_spec],
        out_specs=kv_spec,
        out_shape=jax.ShapeDtypeStruct((b, s, hd), F32),
        scratch_shapes=[pltpu.VMEM((s, LANES), F32), pltpu.VMEM((s, LANES), BF16), pltpu.VMEM((s, LANES), BF16),
                        pltpu.VMEM((s, LANES), F32), pltpu.VMEM((s, LANES), F32),
                        pltpu.VMEM((ng, s, LANES), F32), pltpu.VMEM((ng, s, LANES), F32)],
        compiler_params=_cparams("parallel", "parallel"),
        name="dilated_prompt",
    )(q, q, q, k, v)


def _dilated_sample_kernel(q_ref, kn_ref, vn_ref, ks_ref, kt_ref, vs_ref, vt_ref, o_ref, *, ns, nh, tail):
    d = nh * HEAD_DIM
    ng = len(DIL_GROUPS)
    dil1, dil2 = DIL_GROUPS[1][1], DIL_GROUPS[2][1]
    contract_last = (((1,), (1,)), ((), ()))
    rows_t = 2 * ns * nh
    rows_s = ns * nh

    q = q_ref[...]
    head_of_lane = lax.broadcasted_iota(I32, (nh, d), 1) // HEAD_DIM
    head_of_row = lax.broadcasted_iota(I32, (nh, d), 0)
    pieces = []
    for g in range(ng):
        for qi in range(ns):
            qrow = jnp.broadcast_to(q[qi:qi + 1, g * d:(g + 1) * d], (nh, d))
            pieces.append(jnp.where(head_of_lane == head_of_row, qrow, 0.0))
    qf = jnp.concatenate(pieces, axis=0)
    qb = (qf * ATTN_SCALE).astype(BF16)

    kn, vn = kn_ref[...], vn_ref[...]
    s_new = [jnp.sum(qf * kn[j:j + 1, :], axis=1, keepdims=True) * ATTN_SCALE for j in range(ns)]

    def finish(s_main, s_new_rows, pv_main):
        m = functools.reduce(jnp.maximum, [jnp.max(s_main, axis=1, keepdims=True)] + s_new_rows)
        p = jnp.exp(s_main - m)
        l = jnp.sum(p, axis=1, keepdims=True)
        acc = pv_main(p.astype(BF16))
        for j in range(ns):
            w = jnp.exp(s_new_rows[j] - m)
            l = l + w
            acc = acc + w * vn[j:j + 1, :]
        return acc / l, m + jnp.log(l)

    kt = kt_ref[...].astype(BF16)
    vt = vt_ref[...].astype(BF16)
    s_t = lax.dot_general(qb[0:rows_t], kt, contract_last, preferred_element_type=F32)
    r_idx = lax.broadcasted_iota(I32, (rows_t, tail), 0)
    i_idx = lax.broadcasted_iota(I32, (rows_t, tail), 1)
    g_of = r_idx // (ns * nh)
    q_of = (r_idx // nh) % ns
    vis0 = i_idx >= tail - DIL_SPAN + q_of
    vis1 = (i_idx % dil1) == (q_of % dil1)
    s_t = jnp.where(((g_of == 0) & vis0) | ((g_of != 0) & vis1), s_t, NEG_INF)
    r1 = lax.broadcasted_iota(I32, (rows_t, 1), 0)
    g1, q1 = r1 // (ns * nh), (r1 // nh) % ns
    new_t = [jnp.where(((g1 == 0) & (j <= q1)) | ((g1 != 0) & (j == q1)), s_new[j][0:rows_t], NEG_INF)
             for j in range(ns)]
    o_t, lse_t = finish(s_t, new_t, lambda p: jnp.dot(p, vt, preferred_element_type=F32))

    r2 = lax.broadcasted_iota(I32, (rows_s, 1), 0)
    q2 = r2 // nh
    s_parts = []
    for qi in range(ns):
        ks_q = ks_ref[:, qi * d:(qi + 1) * d].astype(BF16)
        rows = slice(rows_t + qi * nh, rows_t + (qi + 1) * nh)
        s_parts.append(lax.dot_general(qb[rows], ks_q, contract_last, preferred_element_type=F32))
    s_s = jnp.concatenate(s_parts, axis=0)
    new_s = [jnp.where(j == q2, s_new[j][rows_t:], NEG_INF) for j in range(ns)]

    def pv_strided(p):
        parts = []
        for qi in range(ns):
            vs_q = vs_ref[:, qi * d:(qi + 1) * d].astype(BF16)
            parts.append(jnp.dot(p[qi * nh:(qi + 1) * nh], vs_q, preferred_element_type=F32))
        return jnp.concatenate(parts, axis=0)

    o_s, lse_s = finish(s_s, new_s, pv_strided)

    o_all = jnp.concatenate([o_t, o_s], axis=0)
    lse_all = jnp.broadcast_to(jnp.concatenate([lse_t, lse_s], axis=0), o_all.shape)
    own = (lax.broadcasted_iota(I32, o_all.shape, 1) // HEAD_DIM) == (lax.broadcasted_iota(I32, o_all.shape, 0) % nh)
    o_gq = jnp.sum(jnp.where(own, o_all, 0.0).reshape(ng * ns, nh, d), axis=1)
    lse_gq = jnp.sum(jnp.where(own, lse_all, 0.0).reshape(ng * ns, nh, d), axis=1)
    o_ref[...] = _merge_groups([o_gq[g * ns:(g + 1) * ns] for g in range(ng)],
                               [lse_gq[g * ns:(g + 1) * ns] for g in range(ng)])


def _dilated_sample(q, k_new, v_new, cache_k, cache_v):
    nb, ns, d = k_new.shape
    w_buf = cache_k.shape[1]
    ng = len(DIL_GROUPS)
    nh = d // HEAD_DIM
    (win0, dil0), (win1, dil1), (win2, dil2) = DIL_GROUPS
    tail = win1
    assert dil0 == 1 and ns <= dil1 <= dil2 and win0 <= win1 <= w_buf and win2 == w_buf
    assert w_buf % dil2 == 0 and w_buf % tail == 0 and nh % 8 == 0 and q.shape == (nb, ns, ng * d)
    strided = lambda c: c.reshape(nb, w_buf // dil2, dil2 * d)
    new_spec = pl.BlockSpec((None, ns, d), lambda b: (b, 0, 0))
    strided_spec = pl.BlockSpec((None, w_buf // dil2, ns * d), lambda b: (b, 0, 0))
    tail_spec = pl.BlockSpec((None, tail, d), lambda b: (b, w_buf // tail - 1, 0))
    return pl.pallas_call(
        functools.partial(_dilated_sample_kernel, ns=ns, nh=nh, tail=tail),
        grid=(nb,),
        in_specs=[pl.BlockSpec((None, ns, ng * d), lambda b: (b, 0, 0)), new_spec, new_spec,
                  strided_spec, tail_spec, strided_spec, tail_spec],
        out_specs=new_spec,
        out_shape=jax.ShapeDtypeStruct((nb, ns, d), F32),
        compiler_params=_cparams("parallel"),
        name="dilated_sample",
    )(q, k_new, v_new, strided(cache_k), cache_k, strided(cache_v), cache_v)


def _router_kernel(x_ref, g_ref, wr_ref, h_ref, route_ref):
    h = _rmsnorm_rows(x_ref[...], g_ref[...])
    h_ref[...] = h
    logits = jnp.dot(h, wr_ref[...], precision=HIGHEST, preferred_element_type=F32)
    lane = lax.broadcasted_iota(I32, logits.shape, 1)
    lane_f = lane.astype(F32)
    l1 = jnp.where(lane < N_EXPERTS, logits, NEG_INF)
    m1 = jnp.max(l1, axis=1, keepdims=True)
    i1 = jnp.min(jnp.where(l1 == m1, lane_f, float(LANES)), axis=1, keepdims=True)
    l2 = jnp.where(lane_f == i1, NEG_INF, l1)
    m2 = jnp.max(l2, axis=1, keepdims=True)
    i2 = jnp.min(jnp.where(l2 == m2, lane_f, float(LANES)), axis=1, keepdims=True)
    e = jnp.exp(m2 - m1)
    den = 1.0 + e
    packed = jnp.where(lane == 0, i1, jnp.where(lane == 1, i2,
             jnp.where(lane == 2, 1.0 / den, jnp.where(lane == 3, e / den, 0.0))))
    route_ref[...] = packed[:, :route_ref.shape[1]]


ROUTE_COLS = 8


def _router(x, g, w_router_padded):
    n, d = x.shape
    tm = min(TM_ROUTE, n)
    assert n % tm == 0
    return pl.pallas_call(
        _router_kernel,
        grid=(n // tm,),
        in_specs=[pl.BlockSpec((tm, d), lambda i: (i, 0)),
                  pl.BlockSpec((1, d), lambda i: (0, 0)),
                  pl.BlockSpec((d, LANES), lambda i: (0, 0))],
        out_specs=[pl.BlockSpec((tm, d), lambda i: (i, 0)),
                   pl.BlockSpec((tm, ROUTE_COLS), lambda i: (i, 0))],
        out_shape=[jax.ShapeDtypeStruct((n, d), F32), jax.ShapeDtypeStruct((n, ROUTE_COLS), F32)],
        compiler_params=_cparams("parallel"),
        name="moe_router",
    )(x, g.reshape(1, d), w_router_padded)


def _dispatch_kernel(dest_ref, h_hbm, slots_in, slots_out, sem, *, tm):
    del slots_in
    base = pl.program_id(0) * tm

    def row_copy(r, k):
        return pltpu.make_async_copy(h_hbm.at[pl.ds(base + r, 1)],
                                     slots_out.at[pl.ds(dest_ref[0, TOP_K * r + k], 1)], sem)

    def issue(r, c):
        for k in range(TOP_K):
            row_copy(r, k).start()
        return c

    def drain(r, c):
        for k in range(TOP_K):
            row_copy(r, k).wait()
        return c

    lax.fori_loop(0, tm, issue, 0)
    lax.fori_loop(0, tm, drain, 0)


def _dispatch(h, dest, slots):
    n, d = h.shape
    tm = min(TM_ROUTE, n)
    assert n % tm == 0
    return pl.pallas_call(
        functools.partial(_dispatch_kernel, tm=tm),
        grid=(n // tm,),
        in_specs=[pl.BlockSpec((None, 1, TOP_K * tm), lambda i: (i, 0, 0), memory_space=pltpu.SMEM),
                  pl.BlockSpec(memory_space=pl.ANY),
                  pl.BlockSpec(memory_space=pl.ANY)],
        out_specs=pl.BlockSpec(memory_space=pl.ANY),
        out_shape=jax.ShapeDtypeStruct(slots.shape, slots.dtype),
        scratch_shapes=[pltpu.SemaphoreType.DMA(())],
        input_output_aliases={2: 0},
        compiler_params=_cparams("arbitrary"),
        name="moe_dispatch",
    )(dest.reshape(n // tm, 1, TOP_K * tm), h, slots)


def _moe_gu_kernel(exp_ref, valid_ref, first_ref, x_ref, wg_ref, wu_ref, o_ref, wgb_ref, wub_ref):
    del exp_ref
    t = pl.program_id(1)

    @pl.when(first_ref[t] == 1)
    def _():
        wgb_ref[...] = wg_ref[...].astype(BF16)
        wub_ref[...] = wu_ref[...].astype(BF16)

    @pl.when(valid_ref[t] == 1)
    def _():
        x = x_ref[...].astype(BF16)
        gate = jnp.dot(x, wgb_ref[...], preferred_element_type=F32)
        up = jnp.dot(x, wub_ref[...], preferred_element_type=F32)
        o_ref[...] = ((gate * jax.nn.sigmoid(gate)) * up).astype(BF16)

    @pl.when(valid_ref[t] == 0)
    def _():
        o_ref[...] = jnp.zeros(o_ref.shape, o_ref.dtype)


def _moe_gu(slots, w_gu, tile_exp, tile_valid, tile_first):
    r, d = slots.shape
    f = w_gu.shape[2] // 2
    tm, tn = TM_MOE, TN_MOE_GU
    assert r % tm == 0 and f % tn == 0
    nj = f // tn
    return pl.pallas_call(
        _moe_gu_kernel,
        grid_spec=pltpu.PrefetchScalarGridSpec(
            num_scalar_prefetch=3,
            grid=(nj, r // tm),
            in_specs=[pl.BlockSpec((tm, d), lambda j, t, ex, va, fi: (t, 0)),
                      pl.BlockSpec((None, d, tn), lambda j, t, ex, va, fi: (ex[t], 0, j)),
                      pl.BlockSpec((None, d, tn), lambda j, t, ex, va, fi: (ex[t], 0, nj + j))],
            out_specs=pl.BlockSpec((tm, tn), lambda j, t, ex, va, fi: (t, j)),
            scratch_shapes=[pltpu.VMEM((d, tn), BF16), pltpu.VMEM((d, tn), BF16)]),
        out_shape=jax.ShapeDtypeStruct((r, f), BF16),
        compiler_params=_cparams("arbitrary", "arbitrary"),
        name="moe_gate_up",
    )(tile_exp, tile_valid, tile_first, slots, w_gu, w_gu)


def _moe_down_kernel(exp_ref, valid_ref, first_ref, a_ref, wd_ref, o_ref, wdb_ref):
    del exp_ref
    t = pl.program_id(1)

    @pl.when(first_ref[t] == 1)
    def _():
        wdb_ref[...] = wd_ref[...].astype(BF16)

    @pl.when(valid_ref[t] == 1)
    def _():
        o_ref[...] = jnp.dot(a_ref[...], wdb_ref[...], preferred_element_type=F32)

    @pl.when(valid_ref[t] == 0)
    def _():
        o_ref[...] = jnp.zeros(o_ref.shape, o_ref.dtype)


def _moe_down(act, w_down, tile_exp, tile_valid, tile_first):
    r, f = act.shape
    d = w_down.shape[2]
    tm, tn = TM_MOE, TN_MOE_DOWN
    assert r % tm == 0 and d % tn == 0
    return pl.pallas_call(
        _moe_down_kernel,
        grid_spec=pltpu.PrefetchScalarGridSpec(
            num_scalar_prefetch=3,
            grid=(d // tn, r // tm),
            in_specs=[pl.BlockSpec((tm, f), lambda j, t, ex, va, fi: (t, 0)),
                      pl.BlockSpec((None, f, tn), lambda j, t, ex, va, fi: (ex[t], 0, j))],
            out_specs=pl.BlockSpec((tm, tn), lambda j, t, ex, va, fi: (t, j)),
            scratch_shapes=[pltpu.VMEM((f, tn), BF16)]),
        out_shape=jax.ShapeDtypeStruct((r, d), F32),
        compiler_params=_cparams("arbitrary", "arbitrary"),
        name="moe_down",
    )(tile_exp, tile_valid, tile_first, act, w_down)


def _combine_kernel(dest_ref, x_ref, route_ref, rows_hbm, y_ref, buf_ref, sem, *, tm):
    def row_copy(r, k):
        return pltpu.make_async_copy(rows_hbm.at[pl.ds(dest_ref[0, TOP_K * r + k], 1)],
                                     buf_ref.at[k, pl.ds(r, 1)], sem)

    def issue(r, c):
        for k in range(TOP_K):
            row_copy(r, k).start()
        return c

    def drain(r, c):
        for k in range(TOP_K):
            row_copy(r, k).wait()
        return c

    lax.fori_loop(0, tm, issue, 0)
    lax.fori_loop(0, tm, drain, 0)
    route = route_ref[...]
    y_ref[...] = x_ref[...] + (route[:, 2:3] * buf_ref[0] + route[:, 3:4] * buf_ref[1])


def _combine(x, route, dest, rows):
    n, d = x.shape
    tm = min(TM_ROUTE, n)
    assert n % tm == 0
    return pl.pallas_call(
        functools.partial(_combine_kernel, tm=tm),
        grid=(n // tm,),
        in_specs=[pl.BlockSpec((None, 1, TOP_K * tm), lambda i: (i, 0, 0), memory_space=pltpu.SMEM),
                  pl.BlockSpec((tm, d), lambda i: (i, 0)),
                  pl.BlockSpec((tm, ROUTE_COLS), lambda i: (i, 0)),
                  pl.BlockSpec(memory_space=pl.ANY)],
        out_specs=pl.BlockSpec((tm, d), lambda i: (i, 0)),
        out_shape=jax.ShapeDtypeStruct((n, d), F32),
        scratch_shapes=[pltpu.VMEM((TOP_K, tm, d), F32), pltpu.SemaphoreType.DMA(())],
        compiler_params=_cparams("arbitrary"),
        name="moe_combine",
    )(dest.reshape(n // tm, 1, TOP_K * tm), x, route, rows)


def _moe_ffn(xs, g, w_router, w_gu, w_down):
    d = xs[0].shape[1]
    wr = jnp.zeros((d, LANES), F32).at[:, :N_EXPERTS].set(w_router)
    routed = [_router(x, g, wr) for x in xs]
    e_flat = jnp.concatenate([route[:, :TOP_K] for _, route in routed], axis=0).astype(I32).reshape(-1)
    a = e_flat.shape[0]
    onehot = (e_flat[:, None] == jnp.arange(N_EXPERTS, dtype=I32)[None, :]).astype(I32)
    csum = jnp.cumsum(onehot, axis=0)
    rank = jnp.sum(csum * onehot, axis=1) - 1
    counts = csum[-1]
    padded = (counts + TM_MOE - 1) // TM_MOE * TM_MOE
    pends = jnp.cumsum(padded)
    dest = (jnp.sum((pends - padded)[None, :] * onehot, axis=1) + rank).astype(I32)
    n_tiles = -(-a // TM_MOE) + N_EXPERTS
    tile_start = jnp.arange(n_tiles, dtype=I32) * TM_MOE
    tile_exp = jnp.minimum(jnp.sum(tile_start[:, None] >= pends[None, :], axis=1), N_EXPERTS - 1).astype(I32)
    tile_valid = (tile_start < pends[-1]).astype(I32)
    tile_first = jnp.concatenate([jnp.ones((1,), I32), (tile_exp[1:] != tile_exp[:-1]).astype(I32)])

    slots = jnp.zeros((n_tiles * TM_MOE, d), F32)
    dests, off = [], 0
    for h, _ in routed:
        dests.append(dest[off:off + TOP_K * h.shape[0]])
        slots = _dispatch(h, dests[-1], slots)
        off += TOP_K * h.shape[0]
    act = _moe_gu(slots, w_gu, tile_exp, tile_valid, tile_first)
    rows = _moe_down(act, w_down, tile_exp, tile_valid, tile_first)
    return [_combine(x, route, dst, rows) for x, (_, route), dst in zip(xs, routed, dests)]


def kernel(x_prompt, x_sample, cache_a_k, cache_a_v, cache_b_k, cache_b_v, page_table, norm_attn_a, w_qkv_a,
           qnorm_a, knorm_a, w_o_a, norm_kv_b, w_kv_b, knorm_b, norm_attn_b, w_q_b, qnorm_b, w_o_b, norm_ffn,
           w_gu_dense, w_down_dense, w_router, w_gu_moe, w_down_moe):
    b, s, d = x_prompt.shape
    nb, ns, _ = x_sample.shape
    depth, n_a = norm_ffn.shape[0], w_qkv_a.shape[0]
    n_phys, page = cache_a_k.shape[1], cache_a_k.shape[2]
    past = page_table.shape[1] * page
    heads_a = d // HEAD_DIM
    d_b = w_kv_b.shape[1] // 2
    heads_b = d_b // HEAD_DIM
    n_groups = len(DIL_GROUPS)
    w_buf = cache_b_k.shape[1]
    w_max = max(w for w, _ in DIL_GROUPS)

    xp = x_prompt.reshape(b * s, d)
    xs = x_sample.reshape(nb * ns, d)
    tabs_p = _rope_tables(jnp.tile(jnp.arange(s), b))
    tabs_s = _rope_tables(jnp.tile(past + jnp.arange(ns), nb))
    ones_d = jnp.ones((d,), F32)
    ak_p, av_p, ak_s, av_s = [], [], [], []
    for l in range(depth):
        if l < n_a:
            gains = jnp.concatenate([jnp.tile(qnorm_a[l], heads_a), jnp.tile(knorm_a[l], heads_a), ones_d])
            parts = ((d, True), (d, True), (d, False))
            qp, kp, vp = _norm_proj(xp, norm_attn_a[l], w_qkv_a[l], gains, tabs_p, parts)
            qs, ks, vs = _norm_proj(xs, norm_attn_a[l], w_qkv_a[l], gains, tabs_s, parts)
            op = _moba_prompt(qp.reshape(b, s, d), kp.reshape(b, s, d), vp.reshape(b, s, d))
            os_ = _moba_sample(qs.reshape(nb, ns, d), ks.reshape(nb, ns, d), vs.reshape(nb, ns, d),
                               cache_a_k[l].reshape(n_phys, page, d), cache_a_v[l].reshape(n_phys, page, d),
                               page_table)
            xp = _matmul_residual(op.reshape(b * s, d), w_o_a[l], xp)
            xs = _matmul_residual(os_.reshape(nb * ns, d), w_o_a[l], xs)
            ak_p.append(kp)
            av_p.append(vp)
            ak_s.append(ks)
            av_s.append(vs)
        else:
            j = l - n_a
            gains = jnp.tile(qnorm_b[j], n_groups * heads_b)
            parts = ((n_groups * d_b, True),)
            (qp,) = _norm_proj(xp, norm_attn_b[j], w_q_b[j], gains, tabs_p, parts)
            (qs,) = _norm_proj(xs, norm_attn_b[j], w_q_b[j], gains, tabs_s, parts)
            op = _dilated_prompt(qp.reshape(b, s, n_groups * d_b), kb_p.reshape(b, s, d_b), vb_p.reshape(b, s, d_b))
            os_ = _dilated_sample(qs.reshape(nb, ns, n_groups * d_b), kb_s.reshape(nb, ns, d_b),
                                  vb_s.reshape(nb, ns, d_b), cache_b_k.reshape(nb, w_buf, d_b),
                                  cache_b_v.reshape(nb, w_buf, d_b))
            xp = _matmul_residual(op.reshape(b * s, d_b), w_o_b[j], xp)
            xs = _matmul_residual(os_.reshape(nb * ns, d_b), w_o_b[j], xs)
        if l % 2 == 0:
            xp = _dense_ffn(xp, norm_ffn[l], w_gu_dense[l // 2], w_down_dense[l // 2])
            xs = _dense_ffn(xs, norm_ffn[l], w_gu_dense[l // 2], w_down_dense[l // 2])
        else:
            xp, xs = _moe_ffn([xp, xs], norm_ffn[l], w_router[l // 2], w_gu_moe[l // 2], w_down_moe[l // 2])
        if l == n_a - 1:
            gains = jnp.concatenate([jnp.tile(knorm_b, heads_b), jnp.ones((d_b,), F32)])
            parts = ((d_b, True), (d_b, False))
            kb_p, vb_p = _norm_proj(xp, norm_kv_b, w_kv_b, gains, tabs_p, parts)
            kb_s, vb_s = _norm_proj(xs, norm_kv_b, w_kv_b, gains, tabs_s, parts)

    keep = min(w_max, s)
    heads = lambda t, n, h: t.reshape(n_a, *n, h, HEAD_DIM)
    return (xp.reshape(b, s, d), xs.reshape(nb, ns, d),
            heads(jnp.stack(ak_p), (b, s), heads_a), heads(jnp.stack(av_p), (b, s), heads_a),
            kb_p.reshape(b, s, heads_b, HEAD_DIM)[:, s - keep:], vb_p.reshape(b, s, heads_b, HEAD_DIM)[:, s - keep:],
            heads(jnp.stack(ak_s), (nb, ns), heads_a), heads(jnp.stack(av_s), (nb, ns), heads_a),
            kb_s.reshape(nb, ns, heads_b, HEAD_DIM), vb_s.reshape(nb, ns, heads_b, HEAD_DIM))
```

```python
import functools

import jax
import jax.numpy as jnp
from jax import lax
from jax.experimental import pallas as pl
from jax.experimental.pallas import tpu as pltpu

F32 = jnp.float32
BF16 = jnp.bfloat16
I32 = jnp.int32
HIGHEST = lax.Precision.HIGHEST
NEG_INF = float("-inf")

HEAD_DIM = 64
MOBA_BLOCK = 256
MOBA_TOPK = 3
DIL_GROUPS = ((128, 1), (512, 4), (2048, 16))
DIL_SPAN = 128
ROT_DIM = HEAD_DIM // 4
ROPE_THETA = 500000.0
ATTN_SCALE = HEAD_DIM ** -0.5
RMS_EPS = 1e-6
N_EXPERTS = 8
TOP_K = 2

LANES = 128
HEADS_PER_VREG = LANES // HEAD_DIM
VMEM_LIMIT_BYTES = 56 * 1024 * 1024

TM_TOKENS = 512
TN_PROJ = 512
TM_FFN = 1024
TF_FFN = 256
TM_MOE = 512
TN_MOE_GU = 896
TN_MOE_DOWN = 512
TM_ROUTE = 512


def _cparams(*sem):
    return pltpu.CompilerParams(dimension_semantics=sem, vmem_limit_bytes=VMEM_LIMIT_BYTES)


def _rmsnorm_rows(x, g):
    ms = jnp.mean(x * x, axis=-1, keepdims=True)
    return x * lax.rsqrt(ms + RMS_EPS) * g


def _norm_proj_kernel(x_ref, g_ref, w_ref, hg_ref, c_ref, s1_ref, s2_ref, bd_ref, *rest, parts, tn):
    out_refs, h_ref = rest[:len(parts)], rest[len(parts)]
    j = pl.program_id(1)

    @pl.when(j == 0)
    def _():
        h_ref[...] = _rmsnorm_rows(x_ref[...], g_ref[...]).astype(BF16)

    y = jnp.dot(h_ref[...], w_ref[...].astype(BF16), preferred_element_type=F32)

    def head_norm_rope(o_ref):
        c, s1, s2 = c_ref[...], s1_ref[...], s2_ref[...]
        bd = bd_ref[...]
        for cb in range(tn // LANES):
            cols = slice(cb * LANES, (cb + 1) * LANES)
            yc = y[:, cols]
            y2 = yc * yc
            hi = y2.astype(BF16)
            lo = (y2 - hi.astype(F32)).astype(BF16)
            ss = jnp.dot(jnp.concatenate([hi, lo], axis=1), bd, preferred_element_type=F32)
            yn = yc * lax.rsqrt(ss * (1.0 / HEAD_DIM) + RMS_EPS) * hg_ref[:, cols]
            up = pltpu.roll(yn, LANES - ROT_DIM // 2, 1)
            dn = pltpu.roll(yn, ROT_DIM // 2, 1)
            o_ref[:, cols] = yn * c + up * s1 + dn * s2

    for o_ref, (start, ntiles, normed) in zip(out_refs, parts):
        @pl.when((j >= start) & (j < start + ntiles))
        def _(o_ref=o_ref, normed=normed):
            if normed:
                head_norm_rope(o_ref)
            else:
                o_ref[...] = y


def _norm_proj(x, g, w, head_gain, rope_tabs, parts):
    n, d = x.shape
    m = w.shape[1]
    tm, tn = min(TM_TOKENS, n), TN_PROJ
    assert n % tm == 0 and m % tn == 0 and all(p[0] % tn == 0 for p in parts)
    tile_parts, start = [], 0
    for width, normed in parts:
        tile_parts.append((start, width // tn, normed))
        start += width // tn
    c, s1, s2 = rope_tabs
    ones_bd = (jnp.arange(LANES)[:, None] // HEAD_DIM == jnp.arange(LANES)[None, :] // HEAD_DIM)
    bd = jnp.concatenate([ones_bd, ones_bd], axis=0).astype(BF16)

    def out_map(start, ntiles):
        return lambda i, j: (i, jnp.clip(j - start, 0, ntiles - 1))

    tab_spec = pl.BlockSpec((tm, LANES), lambda i, j: (i, 0))
    return pl.pallas_call(
        functools.partial(_norm_proj_kernel, parts=tuple(tile_parts), tn=tn),
        grid=(n // tm, m // tn),
        in_specs=[
            pl.BlockSpec((tm, d), lambda i, j: (i, 0)),
            pl.BlockSpec((1, d), lambda i, j: (0, 0)),
            pl.BlockSpec((d, tn), lambda i, j: (0, j)),
            pl.BlockSpec((1, tn), lambda i, j: (0, j)),
            tab_spec, tab_spec, tab_spec,
            pl.BlockSpec((2 * LANES, LANES), lambda i, j: (0, 0)),
        ],
        out_specs=[pl.BlockSpec((tm, tn), out_map(st, nt)) for st, nt, _ in tile_parts],
        out_shape=[jax.ShapeDtypeStruct((n, width), F32) for width, _ in parts],
        scratch_shapes=[pltpu.VMEM((tm, d), BF16)],
        compiler_params=_cparams("parallel", "arbitrary"),
        name="norm_proj",
    )(x, g.reshape(1, d), w, head_gain.reshape(1, m), c, s1, s2, bd)


def _rope_tables(pos):
    half = ROT_DIM // 2
    inv = ROPE_THETA ** (-2.0 * jnp.arange(half, dtype=F32) / ROT_DIM)
    ang = pos.astype(F32)[:, None] * inv[None, :]
    cos, sin = jnp.cos(ang), jnp.sin(ang)
    n = pos.shape[0]
    zeros = jnp.zeros((n, HEAD_DIM - ROT_DIM), F32)
    zh = jnp.zeros((n, half), F32)
    c = jnp.concatenate([cos, cos, jnp.ones((n, HEAD_DIM - ROT_DIM), F32)], axis=1)
    s1 = jnp.concatenate([-sin, zh, zeros], axis=1)
    s2 = jnp.concatenate([zh, sin, zeros], axis=1)
    tile = lambda t: jnp.tile(t, (1, HEADS_PER_VREG))
    return tile(c), tile(s1), tile(s2)


def _topk_block_mask(gate_t, n_valid):
    nblk = gate_t.shape[0]
    blk_idx = lax.broadcasted_iota(I32, gate_t.shape, 0)
    rank = jnp.zeros(gate_t.shape, I32)
    for m in range(nblk):
        gm = gate_t[m:m + 1, :]
        beats = (gm > gate_t) | ((gm == gate_t) & (m < blk_idx))
        rank = rank + jnp.where(beats & (m < n_valid), 1, 0)
    return jnp.where((blk_idx < n_valid) & (rank < MOBA_TOPK), 1.0, 0.0)


def _moba_prompt_kernel(q_ref, k_ref, v_ref, o_ref, kb_ref, vt_ref, km_ref, *, nblk):
    qi = pl.program_id(2)
    blk = MOBA_BLOCK

    @pl.when(qi == 0)
    def _():
        for n in range(nblk):
            rows = slice(n * blk, (n + 1) * blk)
            half = slice((n % 2) * blk, (n % 2 + 1) * blk)
            kn = k_ref[rows, :]
            kb_ref[n // 2, half, :] = kn.astype(BF16)
            vt_ref[n // 2, :, half] = v_ref[rows, :].T.astype(BF16)
            km_ref[n:n + 1, :] = jnp.mean(kn, axis=0, keepdims=True)

    q_t = q_ref[...].T
    head_of_row = lax.broadcasted_iota(I32, (LANES, blk), 0) // HEAD_DIM
    key_i = lax.broadcasted_iota(I32, (blk, blk), 0)
    qry_i = lax.broadcasted_iota(I32, (blk, blk), 1)
    blk_idx = lax.broadcasted_iota(I32, (nblk, blk), 0)
    km = km_ref[...]

    qs, sel_t = [], []
    for h in range(HEADS_PER_VREG):
        qh = jnp.where(head_of_row == h, q_t, 0.0)
        gate_t = jnp.dot(km, qh, precision=HIGHEST, preferred_element_type=F32)
        sel_t.append(jnp.where(blk_idx == qi, 1.0, _topk_block_mask(gate_t, qi)))
        qs.append((qh * ATTN_SCALE).astype(BF16))

    def block_pair(pi, carry):
        k2 = kb_ref[pi]
        vt2 = vt_ref[pi]
        new = []
        for h in range(HEADS_PER_VREG):
            m, l, a = carry[h]
            s = jnp.dot(k2, qs[h], preferred_element_type=F32)
            halves = []
            for c in range(2):
                j = 2 * pi + c
                vis_j = jnp.max(jnp.where(blk_idx == j, sel_t[h], 0.0), axis=0, keepdims=True)
                causal_slack = jnp.where(j == qi, 0, blk)
                ok = (key_i <= qry_i + causal_slack) & (vis_j > 0.0)
                halves.append(jnp.where(ok, s[c * blk:(c + 1) * blk], NEG_INF))
            s = jnp.concatenate(halves, axis=0)
            m_new = jnp.maximum(m, jnp.max(s, axis=0, keepdims=True))
            alpha = jnp.exp(m - m_new)
            p = jnp.exp(s - m_new)
            l = alpha * l + jnp.sum(p, axis=0, keepdims=True)
            dims = slice(h * HEAD_DIM, (h + 1) * HEAD_DIM)
            a = alpha * a + jnp.dot(vt2[dims, :], p.astype(BF16), preferred_element_type=F32)
            new.append((m_new, l, a))
        return tuple(new)

    init = tuple((jnp.full((1, blk), -1e30, F32), jnp.zeros((1, blk), F32), jnp.zeros((HEAD_DIM, blk), F32))
                 for _ in range(HEADS_PER_VREG))
    final = lax.fori_loop(0, qi // 2 + 1, block_pair, init)
    o_ref[...] = jnp.concatenate([a / l for _, l, a in final], axis=0).T


def _moba_prompt(q, k, v):
    b, s, hd = q.shape
    nblk = s // MOBA_BLOCK
    assert s % (2 * MOBA_BLOCK) == 0 and hd % LANES == 0
    q_spec = pl.BlockSpec((None, MOBA_BLOCK, LANES), lambda bi, hp, qi: (bi, qi, hp))
    kv_spec = pl.BlockSpec((None, s, LANES), lambda bi, hp, qi: (bi, 0, hp))
    return pl.pallas_call(
        functools.partial(_moba_prompt_kernel, nblk=nblk),
        grid=(b, hd // LANES, nblk),
        in_specs=[q_spec, kv_spec, kv_spec],
        out_specs=q_spec,
        out_shape=jax.ShapeDtypeStruct((b, s, hd), F32),
        scratch_shapes=[pltpu.VMEM((nblk // 2, 2 * MOBA_BLOCK, LANES), BF16),
                        pltpu.VMEM((nblk // 2, LANES, 2 * MOBA_BLOCK), BF16),
                        pltpu.VMEM((nblk, LANES), F32)],
        compiler_params=_cparams("parallel", "parallel", "arbitrary"),
        name="moba_prompt",
    )(q, k, v)


def _moba_sample_kernel(pt_ref, q_ref, kn_ref, vn_ref, ka_ref, kbp_ref, va_ref, vbp_ref, o_ref,
                        qf_ref, q2_ref, g_ref, m_ref, l_ref, acc_ref, *, nblk, ns, nh):
    del pt_ref
    n = pl.program_id(1)
    rows = ns * nh
    d = nh * HEAD_DIM
    page = MOBA_BLOCK // 2
    contract_last = (((1,), (1,)), ((), ()))

    @pl.when(n == 0)
    def _():
        q = q_ref[...]
        head_of_lane = lax.broadcasted_iota(I32, (nh, d), 1) // HEAD_DIM
        head_of_row = lax.broadcasted_iota(I32, (nh, d), 0)
        for qi in range(ns):
            piece = jnp.where(head_of_lane == head_of_row, jnp.broadcast_to(q[qi:qi + 1, :], (nh, d)), 0.0)
            hi = piece.astype(BF16)
            qf_ref[qi * nh:(qi + 1) * nh, :] = piece
            q2_ref[qi * nh:(qi + 1) * nh, :] = (piece * ATTN_SCALE).astype(BF16)
            q2_ref[rows + qi * nh:rows + (qi + 1) * nh, :] = (piece - hi.astype(F32)).astype(BF16)

    q2 = q2_ref[...]

    def page_scores(k_page_ref):
        k = k_page_ref[...]
        k_hi = k.astype(BF16)
        k_lo = (k - k_hi.astype(F32)).astype(BF16)
        s2 = jnp.dot(q2, k_hi, preferred_element_type=F32)
        s_lo = jnp.dot(q2[:rows], k_lo, preferred_element_type=F32)
        return s2[:rows], (s2[:rows] + s_lo) * (1.0 / ATTN_SCALE) + s2[rows:]

    sa, ga = page_scores(ka_ref)
    sb, gb = page_scores(kbp_ref)
    s = jnp.concatenate([sa, sb], axis=1)
    gate = (jnp.sum(ga, axis=1, keepdims=True) + jnp.sum(gb, axis=1, keepdims=True)) * (1.0 / MOBA_BLOCK)
    m = jnp.max(s, axis=1, keepdims=True)
    p = jnp.exp(s - m)
    l = jnp.sum(p, axis=1, keepdims=True)
    pb = p.astype(BF16)
    g_ref[n] = jnp.broadcast_to(gate, (rows, LANES))
    m_ref[n] = jnp.broadcast_to(m, (rows, LANES))
    l_ref[n] = jnp.broadcast_to(l, (rows, LANES))
    acc_ref[n] = (lax.dot_general(pb[:, :page], va_ref[...].astype(BF16), contract_last, preferred_element_type=F32)
                  + lax.dot_general(pb[:, page:], vbp_ref[...].astype(BF16), contract_last,
                                    preferred_element_type=F32))

    @pl.when(n == nblk - 1)
    def _():
        sel = []
        for a in range(nblk):
            ga = g_ref[a]
            rank = jnp.zeros((rows, LANES), I32)
            for b in range(nblk):
                gb = g_ref[b]
                beats = (gb > ga) if b > a else (gb >= ga)
                if b != a:
                    rank = rank + jnp.where(beats, 1, 0)
            sel.append(rank < MOBA_TOPK)
        qf = qf_ref[...]
        kn, vn = kn_ref[...], vn_ref[...]
        q_of_row = lax.broadcasted_iota(I32, (rows, 1), 0) // nh
        s_own = [jnp.sum(qf * kn[j:j + 1, :], axis=1, keepdims=True) * ATTN_SCALE for j in range(ns)]
        s_own = [jnp.where(j <= q_of_row, s_own[j], NEG_INF) for j in range(ns)]
        m_blk = [jnp.where(sel[a], m_ref[a], NEG_INF)[:, 0:1] for a in range(nblk)]
        m_all = functools.reduce(jnp.maximum, s_own + m_blk)
        den = jnp.zeros((rows, 1), F32)
        num = jnp.zeros((rows, d), F32)
        for j in range(ns):
            w = jnp.exp(s_own[j] - m_all)
            den = den + w
            num = num + w * vn[j:j + 1, :]
        for a in range(nblk):
            w = jnp.exp(m_blk[a] - m_all)
            den = den + w * l_ref[a][:, 0:1]
            num = num + w * acc_ref[a]
        res = num / den
        head_of_lane = lax.broadcasted_iota(I32, (rows, d), 1) // HEAD_DIM
        head_of_row = lax.broadcasted_iota(I32, (rows, d), 0) % nh
        res = jnp.where(head_of_lane == head_of_row, res, 0.0)
        o_ref[...] = jnp.sum(res.reshape(ns, nh, d), axis=1)


def _moba_sample(q, k_new, v_new, pool_k, pool_v, page_table):
    nb, ns, d = q.shape
    n_phys, page, nh, _ = pool_k.shape
    nblk = page_table.shape[1] * page // MOBA_BLOCK
    assert 2 * page == MOBA_BLOCK and page_table.shape[1] * page == nblk * MOBA_BLOCK
    assert ns <= MOBA_BLOCK and nh % 8 == 0 and d == nh * HEAD_DIM
    rows = ns * nh
    pool_k = jnp.transpose(pool_k, (0, 2, 3, 1)).reshape(n_phys, d, page)
    pool_v = jnp.transpose(pool_v, (0, 2, 3, 1)).reshape(n_phys, d, page)
    new_spec = pl.BlockSpec((None, ns, d), lambda b, n, pt: (b, 0, 0))
    page_a = pl.BlockSpec((None, d, page), lambda b, n, pt: (pt[b, 2 * n], 0, 0))
    page_b = pl.BlockSpec((None, d, page), lambda b, n, pt: (pt[b, 2 * n + 1], 0, 0))
    return pl.pallas_call(
        functools.partial(_moba_sample_kernel, nblk=nblk, ns=ns, nh=nh),
        grid_spec=pltpu.PrefetchScalarGridSpec(
            num_scalar_prefetch=1,
            grid=(nb, nblk),
            in_specs=[new_spec, new_spec, new_spec, page_a, page_b, page_a, page_b],
            out_specs=new_spec,
            scratch_shapes=[
                pltpu.VMEM((rows, d), F32), pltpu.VMEM((2 * rows, d), BF16),
                pltpu.VMEM((nblk, rows, LANES), F32), pltpu.VMEM((nblk, rows, LANES), F32),
                pltpu.VMEM((nblk, rows, LANES), F32), pltpu.VMEM((nblk, rows, d), F32),
            ]),
        out_shape=jax.ShapeDtypeStruct((nb, ns, d), F32),
        compiler_params=_cparams("parallel", "arbitrary"),
        name="moba_sample",
    )(page_table, q, k_new, v_new, pool_k, pool_k, pool_v, pool_v)


def _matmul_residual_kernel(a_ref, w_ref, r_ref, o_ref):
    o_ref[...] = r_ref[...] + jnp.dot(a_ref[...].astype(BF16), w_ref[...].astype(BF16),
                                      preferred_element_type=F32)


def _matmul_residual(a, w, res):
    n, k = a.shape
    m = w.shape[1]
    tm = min(TM_TOKENS, n)
    assert n % tm == 0
    return pl.pallas_call(
        _matmul_residual_kernel,
        grid=(n // tm,),
        in_specs=[pl.BlockSpec((tm, k), lambda i: (i, 0)),
                  pl.BlockSpec((k, m), lambda i: (0, 0)),
                  pl.BlockSpec((tm, m), lambda i: (i, 0))],
        out_specs=pl.BlockSpec((tm, m), lambda i: (i, 0)),
        out_shape=jax.ShapeDtypeStruct((n, m), F32),
        compiler_params=_cparams("parallel"),
        name="matmul_residual",
    )(a, w, res)


def _dense_ffn_kernel(x_ref, g_ref, wg_ref, wu_ref, wd_ref, o_ref, h_ref):
    @pl.when(pl.program_id(1) == 0)
    def _():
        x = x_ref[...]
        h_ref[...] = _rmsnorm_rows(x, g_ref[...]).astype(BF16)
        o_ref[...] = x
    h = h_ref[...]
    gate = jnp.dot(h, wg_ref[...].astype(BF16), preferred_element_type=F32)
    up = jnp.dot(h, wu_ref[...].astype(BF16), preferred_element_type=F32)
    act = (gate * jax.nn.sigmoid(gate)) * up
    o_ref[...] += jnp.dot(act.astype(BF16), wd_ref[...].astype(BF16), preferred_element_type=F32)


def _dense_ffn(x, g, w_gu, w_down):
    n, d = x.shape
    f = w_down.shape[0]
    tm, tf = min(TM_FFN, n), TF_FFN
    assert n % tm == 0 and f % tf == 0
    nf = f // tf
    return pl.pallas_call(
        _dense_ffn_kernel,
        grid=(n // tm, nf),
        in_specs=[pl.BlockSpec((tm, d), lambda i, c: (i, 0)),
                  pl.BlockSpec((1, d), lambda i, c: (0, 0)),
                  pl.BlockSpec((d, tf), lambda i, c: (0, c)),
                  pl.BlockSpec((d, tf), lambda i, c: (0, nf + c)),
                  pl.BlockSpec((tf, d), lambda i, c: (c, 0))],
        out_specs=pl.BlockSpec((tm, d), lambda i, c: (i, 0)),
        out_shape=jax.ShapeDtypeStruct((n, d), F32),
        scratch_shapes=[pltpu.VMEM((tm, d), BF16)],
        compiler_params=_cparams("parallel", "arbitrary"),
        name="dense_ffn",
    )(x, g.reshape(1, d), w_gu, w_gu, w_down)


def _merge_groups(outs, lses):
    m = functools.reduce(jnp.maximum, lses)
    ws = [jnp.exp(l - m) for l in lses]
    num = functools.reduce(lambda a, b: a + b, [w * o for w, o in zip(ws, outs)])
    return num / functools.reduce(lambda a, b: a + b, ws)


def _dilated_prompt_kernel(q0_ref, q1_ref, q2_ref, k_ref, v_ref, o_ref,
                           qd_ref, kd_ref, vd_ref, od_ref, ld_ref, og_ref, lg_ref, *, s):
    span = DIL_SPAN
    ntiles = s // span
    contract_last = (((1,), (1,)), ((), ()))
    lane = lax.broadcasted_iota(I32, (span, LANES), 1)
    row = lax.broadcasted_iota(I32, (span, span), 0)
    col = lax.broadcasted_iota(I32, (span, span), 1)

    for g, (q_ref, (_, dil)) in enumerate(zip((q0_ref, q1_ref, q2_ref), DIL_GROUPS)):
        sub = s // dil
        for r in range(dil):
            src = pl.ds(r, sub, stride=dil) if dil > 1 else pl.ds(0, s)
            dst = slice(r * sub, (r + 1) * sub)
            qd_ref[dst, :] = q_ref[src, :] * ATTN_SCALE
            kd_ref[dst, :] = k_ref[src, :].astype(BF16)
            vd_ref[dst, :] = v_ref[src, :].astype(BF16)
        tiles_per_class = sub // span

        def tile(t, carry, tiles_per_class=tiles_per_class):
            cur = pl.ds(pl.multiple_of(t * span, span), span)
            prev = pl.ds(pl.multiple_of(jnp.maximum(t - 1, 0) * span, span), span)
            prev_off = jnp.where((t % tiles_per_class) != 0, 0, span)
            qt = qd_ref[cur, :]
            kc, vc = kd_ref[cur, :], vd_ref[cur, :]
            kp, vp = kd_ref[prev, :], vd_ref[prev, :]
            vcat = jnp.concatenate([vp, vc], axis=0)
            outs, lses = [], []
            for h in range(HEADS_PER_VREG):
                qh = jnp.where((lane // HEAD_DIM) == h, qt, 0.0).astype(BF16)
                sp = lax.dot_general(qh, kp, contract_last, preferred_element_type=F32)
                sp = jnp.where(col >= row + prev_off, sp, NEG_INF)
                sc = lax.dot_general(qh, kc, contract_last, preferred_element_type=F32)
                sc = jnp.where(col <= row, sc, NEG_INF)
                sa = jnp.concatenate([sp, sc], axis=1)
                m = jnp.max(sa, axis=1, keepdims=True)
                p = jnp.exp(sa - m)
                l = jnp.sum(p, axis=1, keepdims=True)
                outs.append(jnp.dot(p.astype(BF16), vcat, preferred_element_type=F32) / l)
                lses.append(jnp.broadcast_to(m + jnp.log(l), (span, LANES)))
            first = (lane // HEAD_DIM) == 0
            od_ref[cur, :] = jnp.where(first, outs[0], outs[1])
            ld_ref[cur, :] = jnp.where(first, lses[0], lses[1])
            return carry

        lax.fori_loop(0, ntiles, tile, 0, unroll=4)
        for r in range(dil):
            dst = pl.ds(r, sub, stride=dil) if dil > 1 else pl.ds(0, s)
            src = slice(r * sub, (r + 1) * sub)
            og_ref[g, dst, :] = od_ref[src, :]
            lg_ref[g, dst, :] = ld_ref[src, :]

    ng = len(DIL_GROUPS)
    o_ref[...] = _merge_groups([og_ref[g] for g in range(ng)], [lg_ref[g] for g in range(ng)])


def _dilated_prompt(q, k, v):
    b, s, hd = k.shape
    ng = len(DIL_GROUPS)
    assert q.shape == (b, s, ng * hd) and hd % LANES == 0
    assert all(s % (DIL_SPAN * dil) == 0 and win // dil == DIL_SPAN for win, dil in DIL_GROUPS)
    hp_blocks = hd // LANES

    def q_spec(g):
        return pl.BlockSpec((None, s, LANES), lambda bi, hp: (bi, 0, g * hp_blocks + hp))

    kv_spec = pl.BlockSpec((None, s, LANES), lambda bi, hp: (bi, 0, hp))
    return pl.pallas_call(
        functools.partial(_dilated_prompt_kernel, s=s),
        grid=(b, hp_blocks),
        in_specs=[q_spec(0), q_spec(1), q_spec(2), kv_spec, kv_spec],
        out_specs=kv_spec,
        out_shape=jax.ShapeDtypeStruct((b, s, hd), F32),
        scratch_shapes=[pltpu.VMEM((s, LANES), F32), pltpu.VMEM((s, LANES), BF16), pltpu.VMEM((s, LANES), BF16),
                        pltpu.VMEM((s, LANES), F32), pltpu.VMEM((s, LANES), F32),
                        pltpu.VMEM((ng, s, LANES), F32), pltpu.VMEM((ng, s, LANES), F32)],
        compiler_params=_cparams("parallel", "parallel"),
        name="dilated_prompt",
    )(q, q, q, k, v)


def _dilated_sample_kernel(q_ref, kn_ref, vn_ref, kt_ref, vt_ref, o_ref, *, ns, nh, w_buf):
    d = nh * HEAD_DIM
    ng = len(DIL_GROUPS)
    rows = ng * ns * nh
    contract_last = (((1,), (1,)), ((), ()))

    q = q_ref[...]
    head_of_lane = lax.broadcasted_iota(I32, (nh, d), 1) // HEAD_DIM
    head_of_row = lax.broadcasted_iota(I32, (nh, d), 0)
    pieces = []
    for g in range(ng):
        for qi in range(ns):
            qrow = jnp.broadcast_to(q[qi:qi + 1, g * d:(g + 1) * d], (nh, d))
            pieces.append(jnp.where(head_of_lane == head_of_row, qrow, 0.0))
    qf = jnp.concatenate(pieces, axis=0)
    qb = (qf * ATTN_SCALE).astype(BF16)

    kn, vn = kn_ref[...], vn_ref[...]
    s_new = [jnp.sum(qf * kn[j:j + 1, :], axis=1, keepdims=True) * ATTN_SCALE for j in range(ns)]

    r1 = lax.broadcasted_iota(I32, (rows, 1), 0)
    g1, q1 = r1 // (ns * nh), (r1 // nh) % ns
    win_r = jnp.zeros((rows, 1), I32)
    low_bits_r = jnp.zeros((rows, 1), I32)
    for g, (win, dil) in enumerate(DIL_GROUPS):
        win_r = jnp.where(g1 == g, win, win_r)
        low_bits_r = jnp.where(g1 == g, dil - 1, low_bits_r)

    s = jnp.dot(qb, kt_ref[...].astype(BF16), preferred_element_type=F32)
    dist = (w_buf + q1) - lax.broadcasted_iota(I32, (rows, w_buf), 1)
    s = jnp.where(((dist & low_bits_r) == 0) & (dist <= win_r), s, NEG_INF)
    s_new = [jnp.where((q1 >= j) & (((q1 - j) & low_bits_r) == 0) & (q1 - j <= win_r), s_new[j], NEG_INF)
             for j in range(ns)]
    m = functools.reduce(jnp.maximum, [jnp.max(s, axis=1, keepdims=True)] + s_new)
    p = jnp.exp(s - m)
    l = jnp.sum(p, axis=1, keepdims=True)
    acc = lax.dot_general(p.astype(BF16), vt_ref[...].astype(BF16), contract_last, preferred_element_type=F32)
    for j in range(ns):
        w = jnp.exp(s_new[j] - m)
        l = l + w
        acc = acc + w * vn[j:j + 1, :]
    o_all = acc / l
    lse_all = jnp.broadcast_to(m + jnp.log(l), o_all.shape)

    own = (lax.broadcasted_iota(I32, o_all.shape, 1) // HEAD_DIM) == (lax.broadcasted_iota(I32, o_all.shape, 0) % nh)
    o_gq = jnp.sum(jnp.where(own, o_all, 0.0).reshape(ng * ns, nh, d), axis=1)
    lse_gq = jnp.sum(jnp.where(own, lse_all, 0.0).reshape(ng * ns, nh, d), axis=1)
    o_ref[...] = _merge_groups([o_gq[g * ns:(g + 1) * ns] for g in range(ng)],
                               [lse_gq[g * ns:(g + 1) * ns] for g in range(ng)])


def _dilated_sample(q, k_new, v_new, cache_k, cache_v):
    nb, ns, d = k_new.shape
    _, w_buf, nh, _ = cache_k.shape
    ng = len(DIL_GROUPS)
    assert all(dil & (dil - 1) == 0 for _, dil in DIL_GROUPS)
    assert d == nh * HEAD_DIM and nh % 8 == 0 and w_buf % LANES == 0 and q.shape == (nb, ns, ng * d)
    transposed = lambda c: jnp.transpose(c, (0, 2, 3, 1)).reshape(nb, d, w_buf)
    new_spec = pl.BlockSpec((None, ns, d), lambda b: (b, 0, 0))
    buf_spec = pl.BlockSpec((None, d, w_buf), lambda b: (b, 0, 0))
    return pl.pallas_call(
        functools.partial(_dilated_sample_kernel, ns=ns, nh=nh, w_buf=w_buf),
        grid=(nb,),
        in_specs=[pl.BlockSpec((None, ns, ng * d), lambda b: (b, 0, 0)), new_spec, new_spec, buf_spec, buf_spec],
        out_specs=new_spec,
        out_shape=jax.ShapeDtypeStruct((nb, ns, d), F32),
        compiler_params=_cparams("parallel"),
        name="dilated_sample",
    )(q, k_new, v_new, transposed(cache_k), transposed(cache_v))


def _router_kernel(x_ref, g_ref, wr_ref, h_ref, route_ref):
    h = _rmsnorm_rows(x_ref[...], g_ref[...])
    h_ref[...] = h
    logits = jnp.dot(h, wr_ref[...], precision=HIGHEST, preferred_element_type=F32)
    lane = lax.broadcasted_iota(I32, logits.shape, 1)
    lane_f = lane.astype(F32)
    l1 = jnp.where(lane < N_EXPERTS, logits, NEG_INF)
    m1 = jnp.max(l1, axis=1, keepdims=True)
    i1 = jnp.min(jnp.where(l1 == m1, lane_f, float(LANES)), axis=1, keepdims=True)
    l2 = jnp.where(lane_f == i1, NEG_INF, l1)
    m2 = jnp.max(l2, axis=1, keepdims=True)
    i2 = jnp.min(jnp.where(l2 == m2, lane_f, float(LANES)), axis=1, keepdims=True)
    e = jnp.exp(m2 - m1)
    den = 1.0 + e
    packed = jnp.where(lane == 0, i1, jnp.where(lane == 1, i2,
             jnp.where(lane == 2, 1.0 / den, jnp.where(lane == 3, e / den, 0.0))))
    route_ref[...] = packed[:, :route_ref.shape[1]]


ROUTE_COLS = 8


def _router(x, g, w_router_padded):
    n, d = x.shape
    tm = min(TM_ROUTE, n)
    assert n % tm == 0
    return pl.pallas_call(
        _router_kernel,
        grid=(n // tm,),
        in_specs=[pl.BlockSpec((tm, d), lambda i: (i, 0)),
                  pl.BlockSpec((1, d), lambda i: (0, 0)),
                  pl.BlockSpec((d, LANES), lambda i: (0, 0))],
        out_specs=[pl.BlockSpec((tm, d), lambda i: (i, 0)),
                   pl.BlockSpec((tm, ROUTE_COLS), lambda i: (i, 0))],
        out_shape=[jax.ShapeDtypeStruct((n, d), F32), jax.ShapeDtypeStruct((n, ROUTE_COLS), F32)],
        compiler_params=_cparams("parallel"),
        name="moe_router",
    )(x, g.reshape(1, d), w_router_padded)


def _dispatch_kernel(dest_ref, h_ref, slots_in, slots_out, sem, *, tm):
    del slots_in

    def row_copy(r, k):
        return pltpu.make_async_copy(h_ref.at[pl.ds(r, 1)],
                                     slots_out.at[pl.ds(dest_ref[0, TOP_K * r + k], 1)], sem)

    def issue(r, c):
        for k in range(TOP_K):
            row_copy(r, k).start()
        return c

    def drain(r, c):
        for k in range(TOP_K):
            row_copy(r, k).wait()
        return c

    lax.fori_loop(0, tm, issue, 0)
    lax.fori_loop(0, tm, drain, 0)


def _dispatch(h, dest, slots):
    n, d = h.shape
    tm = min(TM_ROUTE, n)
    assert n % tm == 0
    return pl.pallas_call(
        functools.partial(_dispatch_kernel, tm=tm),
        grid=(n // tm,),
        in_specs=[pl.BlockSpec((None, 1, TOP_K * tm), lambda i: (i, 0, 0), memory_space=pltpu.SMEM),
                  pl.BlockSpec((tm, d), lambda i: (i, 0)),
                  pl.BlockSpec(memory_space=pl.ANY)],
        out_specs=pl.BlockSpec(memory_space=pl.ANY),
        out_shape=jax.ShapeDtypeStruct(slots.shape, slots.dtype),
        scratch_shapes=[pltpu.SemaphoreType.DMA(())],
        input_output_aliases={2: 0},
        compiler_params=_cparams("arbitrary"),
        name="moe_dispatch",
    )(dest.reshape(n // tm, 1, TOP_K * tm), h, slots)


def _moe_gu_kernel(exp_ref, valid_ref, first_ref, x_ref, wg_ref, wu_ref, o_ref, wgb_ref, wub_ref):
    del exp_ref
    t = pl.program_id(1)

    @pl.when(first_ref[t] == 1)
    def _():
        wgb_ref[...] = wg_ref[...].astype(BF16)
        wub_ref[...] = wu_ref[...].astype(BF16)

    @pl.when(valid_ref[t] == 1)
    def _():
        x = x_ref[...].astype(BF16)
        gate = jnp.dot(x, wgb_ref[...], preferred_element_type=F32)
        up = jnp.dot(x, wub_ref[...], preferred_element_type=F32)
        o_ref[...] = ((gate * jax.nn.sigmoid(gate)) * up).astype(BF16)

    @pl.when(valid_ref[t] == 0)
    def _():
        o_ref[...] = jnp.zeros(o_ref.shape, o_ref.dtype)


def _moe_gu(slots, w_gu, tile_exp, tile_valid, tile_first):
    r, d = slots.shape
    f = w_gu.shape[2] // 2
    tm, tn = TM_MOE, TN_MOE_GU
    assert r % tm == 0 and f % tn == 0
    nj = f // tn
    return pl.pallas_call(
        _moe_gu_kernel,
        grid_spec=pltpu.PrefetchScalarGridSpec(
            num_scalar_prefetch=3,
            grid=(nj, r // tm),
            in_specs=[pl.BlockSpec((tm, d), lambda j, t, ex, va, fi: (t, 0)),
                      pl.BlockSpec((None, d, tn), lambda j, t, ex, va, fi: (ex[t], 0, j)),
                      pl.BlockSpec((None, d, tn), lambda j, t, ex, va, fi: (ex[t], 0, nj + j))],
            out_specs=pl.BlockSpec((tm, tn), lambda j, t, ex, va, fi: (t, j)),
            scratch_shapes=[pltpu.VMEM((d, tn), BF16), pltpu.VMEM((d, tn), BF16)]),
        out_shape=jax.ShapeDtypeStruct((r, f), BF16),
        compiler_params=_cparams("arbitrary", "arbitrary"),
        name="moe_gate_up",
    )(tile_exp, tile_valid, tile_first, slots, w_gu, w_gu)


def _moe_down_kernel(exp_ref, valid_ref, first_ref, a_ref, wd_ref, o_ref, wdb_ref):
    del exp_ref
    t = pl.program_id(1)

    @pl.when(first_ref[t] == 1)
    def _():
        wdb_ref[...] = wd_ref[...].astype(BF16)

    @pl.when(valid_ref[t] == 1)
    def _():
        o_ref[...] = jnp.dot(a_ref[...], wdb_ref[...], preferred_element_type=F32)

    @pl.when(valid_ref[t] == 0)
    def _():
        o_ref[...] = jnp.zeros(o_ref.shape, o_ref.dtype)


def _moe_down(act, w_down, tile_exp, tile_valid, tile_first):
    r, f = act.shape
    d = w_down.shape[2]
    tm, tn = TM_MOE, TN_MOE_DOWN
    assert r % tm == 0 and d % tn == 0
    return pl.pallas_call(
        _moe_down_kernel,
        grid_spec=pltpu.PrefetchScalarGridSpec(
            num_scalar_prefetch=3,
            grid=(d // tn, r // tm),
            in_specs=[pl.BlockSpec((tm, f), lambda j, t, ex, va, fi: (t, 0)),
                      pl.BlockSpec((None, f, tn), lambda j, t, ex, va, fi: (ex[t], 0, j))],
            out_specs=pl.BlockSpec((tm, tn), lambda j, t, ex, va, fi: (t, j)),
            scratch_shapes=[pltpu.VMEM((f, tn), BF16)]),
        out_shape=jax.ShapeDtypeStruct((r, d), F32),
        compiler_params=_cparams("arbitrary", "arbitrary"),
        name="moe_down",
    )(tile_exp, tile_valid, tile_first, act, w_down)


def _combine_kernel(dest_ref, x_ref, route_ref, rows_hbm, y_ref, buf_ref, sem, *, tm):
    def row_copy(r, k):
        return pltpu.make_async_copy(rows_hbm.at[pl.ds(dest_ref[0, TOP_K * r + k], 1)],
                                     buf_ref.at[k, pl.ds(r, 1)], sem)

    def issue(r, c):
        for k in range(TOP_K):
            row_copy(r, k).start()
        return c

    def drain(r, c):
        for k in range(TOP_K):
            row_copy(r, k).wait()
        return c

    lax.fori_loop(0, tm, issue, 0)
    lax.fori_loop(0, tm, drain, 0)
    route = route_ref[...]
    y_ref[...] = x_ref[...] + (route[:, 2:3] * buf_ref[0] + route[:, 3:4] * buf_ref[1])


def _combine(x, route, dest, rows):
    n, d = x.shape
    tm = min(TM_ROUTE, n)
    assert n % tm == 0
    return pl.pallas_call(
        functools.partial(_combine_kernel, tm=tm),
        grid=(n // tm,),
        in_specs=[pl.BlockSpec((None, 1, TOP_K * tm), lambda i: (i, 0, 0), memory_space=pltpu.SMEM),
                  pl.BlockSpec((tm, d), lambda i: (i, 0)),
                  pl.BlockSpec((tm, ROUTE_COLS), lambda i: (i, 0)),
                  pl.BlockSpec(memory_space=pl.ANY)],
        out_specs=pl.BlockSpec((tm, d), lambda i: (i, 0)),
        out_shape=jax.ShapeDtypeStruct((n, d), F32),
        scratch_shapes=[pltpu.VMEM((TOP_K, tm, d), F32), pltpu.SemaphoreType.DMA(())],
        compiler_params=_cparams("arbitrary"),
        name="moe_combine",
    )(dest.reshape(n // tm, 1, TOP_K * tm), x, route, rows)


def _moe_ffn(xs, g, w_router, w_gu, w_down):
    d = xs[0].shape[1]
    wr = jnp.zeros((d, LANES), F32).at[:, :N_EXPERTS].set(w_router)
    routed = [_router(x, g, wr) for x in xs]
    e_flat = jnp.concatenate([route[:, :TOP_K] for _, route in routed], axis=0).astype(I32).reshape(-1)
    a = e_flat.shape[0]
    onehot = (e_flat[:, None] == jnp.arange(N_EXPERTS, dtype=I32)[None, :]).astype(I32)
    csum = jnp.cumsum(onehot, axis=0)
    rank = jnp.sum(csum * onehot, axis=1) - 1
    counts = csum[-1]
    padded = (counts + TM_MOE - 1) // TM_MOE * TM_MOE
    pends = jnp.cumsum(padded)
    dest = (jnp.sum((pends - padded)[None, :] * onehot, axis=1) + rank).astype(I32)
    n_tiles = -(-a // TM_MOE) + N_EXPERTS
    tile_start = jnp.arange(n_tiles, dtype=I32) * TM_MOE
    tile_exp = jnp.minimum(jnp.sum(tile_start[:, None] >= pends[None, :], axis=1), N_EXPERTS - 1).astype(I32)
    tile_valid = (tile_start < pends[-1]).astype(I32)
    tile_first = jnp.concatenate([jnp.ones((1,), I32), (tile_exp[1:] != tile_exp[:-1]).astype(I32)])

    slots = jnp.zeros((n_tiles * TM_MOE, d), F32)
    dests, off = [], 0
    for h, _ in routed:
        dests.append(dest[off:off + TOP_K * h.shape[0]])
        slots = _dispatch(h, dests[-1], slots)
        off += TOP_K * h.shape[0]
    act = _moe_gu(slots, w_gu, tile_exp, tile_valid, tile_first)
    rows = _moe_down(act, w_down, tile_exp, tile_valid, tile_first)
    return [_combine(x, route, dst, rows) for x, (_, route), dst in zip(xs, routed, dests)]


def kernel(x_prompt, x_sample, cache_a_k, cache_a_v, cache_b_k, cache_b_v, page_table, norm_attn_a, w_qkv_a,
           qnorm_a, knorm_a, w_o_a, norm_kv_b, w_kv_b, knorm_b, norm_attn_b, w_q_b, qnorm_b, w_o_b, norm_ffn,
           w_gu_dense, w_down_dense, w_router, w_gu_moe, w_down_moe):
    b, s, d = x_prompt.shape
    nb, ns, _ = x_sample.shape
    depth, n_a = norm_ffn.shape[0], w_qkv_a.shape[0]
    n_phys, page = cache_a_k.shape[1], cache_a_k.shape[2]
    past = page_table.shape[1] * page
    heads_a = d // HEAD_DIM
    d_b = w_kv_b.shape[1] // 2
    heads_b = d_b // HEAD_DIM
    n_groups = len(DIL_GROUPS)
    w_buf = cache_b_k.shape[1]
    w_max = max(w for w, _ in DIL_GROUPS)

    xp = x_prompt.reshape(b * s, d)
    xs = x_sample.reshape(nb * ns, d)
    tabs_p = _rope_tables(jnp.tile(jnp.arange(s), b))
    tabs_s = _rope_tables(jnp.tile(past + jnp.arange(ns), nb))
    ones_d = jnp.ones((d,), F32)
    ak_p, av_p, ak_s, av_s = [], [], [], []
    for l in range(depth):
        if l < n_a:
            gains = jnp.concatenate([jnp.tile(qnorm_a[l], heads_a), jnp.tile(knorm_a[l], heads_a), ones_d])
            parts = ((d, True), (d, True), (d, False))
            qp, kp, vp = _norm_proj(xp, norm_attn_a[l], w_qkv_a[l], gains, tabs_p, parts)
            qs, ks, vs = _norm_proj(xs, norm_attn_a[l], w_qkv_a[l], gains, tabs_s, parts)
            op = _moba_prompt(qp.reshape(b, s, d), kp.reshape(b, s, d), vp.reshape(b, s, d))
            os_ = _moba_sample(qs.reshape(nb, ns, d), ks.reshape(nb, ns, d), vs.reshape(nb, ns, d),
                               cache_a_k[l], cache_a_v[l], page_table)
            xp = _matmul_residual(op.reshape(b * s, d), w_o_a[l], xp)
            xs = _matmul_residual(os_.reshape(nb * ns, d), w_o_a[l], xs)
            ak_p.append(kp)
            av_p.append(vp)
            ak_s.append(ks)
            av_s.append(vs)
        else:
            j = l - n_a
            gains = jnp.tile(qnorm_b[j], n_groups * heads_b)
            parts = ((n_groups * d_b, True),)
            (qp,) = _norm_proj(xp, norm_attn_b[j], w_q_b[j], gains, tabs_p, parts)
            (qs,) = _norm_proj(xs, norm_attn_b[j], w_q_b[j], gains, tabs_s, parts)
            op = _dilated_prompt(qp.reshape(b, s, n_groups * d_b), kb_p.reshape(b, s, d_b), vb_p.reshape(b, s, d_b))
            os_ = _dilated_sample(qs.reshape(nb, ns, n_groups * d_b), kb_s.reshape(nb, ns, d_b),
                                  vb_s.reshape(nb, ns, d_b), cache_b_k, cache_b_v)
            xp = _matmul_residual(op.reshape(b * s, d_b), w_o_b[j], xp)
            xs = _matmul_residual(os_.reshape(nb * ns, d_b), w_o_b[j], xs)
        if l % 2 == 0:
            xp = _dense_ffn(xp, norm_ffn[l], w_gu_dense[l // 2], w_down_dense[l // 2])
            xs = _dense_ffn(xs, norm_ffn[l], w_gu_dense[l // 2], w_down_dense[l // 2])
        else:
            xp, xs = _moe_ffn([xp, xs], norm_ffn[l], w_router[l // 2], w_gu_moe[l // 2], w_down_moe[l // 2])
        if l == n_a - 1:
            gains = jnp.concatenate([jnp.tile(knorm_b, heads_b), jnp.ones((d_b,), F32)])
            parts = ((d_b, True), (d_b, False))
            kb_p, vb_p = _norm_proj(xp, norm_kv_b, w_kv_b, gains, tabs_p, parts)
            kb_s, vb_s = _norm_proj(xs, norm_kv_b, w_kv_b, gains, tabs_s, parts)

    keep = min(w_max, s)
    heads = lambda t, n, h: t.reshape(n_a, *n, h, HEAD_DIM)
    return (xp.reshape(b, s, d), xs.reshape(nb, ns, d),
            heads(jnp.stack(ak_p), (b, s), heads_a), heads(jnp.stack(av_p), (b, s), heads_a),
            kb_p.reshape(b, s, heads_b, HEAD_DIM)[:, s - keep:], vb_p.reshape(b, s, heads_b, HEAD_DIM)[:, s - keep:],
            heads(jnp.stack(ak_s), (nb, ns), heads_a), heads(jnp.stack(av_s), (nb, ns), heads_a),
            kb_s.reshape(nb, ns, heads_b, HEAD_DIM), vb_s.reshape(nb, ns, heads_b, HEAD_DIM))
```

```python
import functools

import jax
import jax.numpy as jnp
from jax import lax
from jax.experimental import pallas as pl
from jax.experimental.pallas import tpu as pltpu

F32 = jnp.float32
BF16 = jnp.bfloat16
I32 = jnp.int32
HIGHEST = lax.Precision.HIGHEST
NEG_INF = float("-inf")

HEAD_DIM = 64
MOBA_BLOCK = 256
MOBA_TOPK = 3
DIL_GROUPS = ((128, 1), (512, 4), (2048, 16))
DIL_SPAN = 128
ROT_DIM = HEAD_DIM // 4
ROPE_THETA = 500000.0
ATTN_SCALE = HEAD_DIM ** -0.5
RMS_EPS = 1e-6
N_EXPERTS = 8
TOP_K = 2

LANES = 128
HEADS_PER_VREG = LANES // HEAD_DIM
VMEM_LIMIT_BYTES = 56 * 1024 * 1024

TM_TOKENS = 512
TN_PROJ = 512
TM_FFN = 1024
TF_FFN = 256
TM_MOE = 512
TN_MOE_GU = 896
TN_MOE_DOWN = 512
TM_ROUTE = 512


def _cparams(*sem):
    return pltpu.CompilerParams(dimension_semantics=sem, vmem_limit_bytes=VMEM_LIMIT_BYTES)


def _rmsnorm_rows(x, g):
    ms = jnp.mean(x * x, axis=-1, keepdims=True)
    return x * lax.rsqrt(ms + RMS_EPS) * g


MXU_COLS = 256


def _norm_proj_kernel(x_ref, g_ref, w_ref, hg_ref, c_ref, s1_ref, s2_ref, bd_ref, *rest, parts, tn):
    h_ref = rest[-1]
    out_refs = iter(rest[:-1])
    j = pl.program_id(1)

    @pl.when(j == 0)
    def _():
        h_ref[...] = _rmsnorm_rows(x_ref[...], g_ref[...]).astype(BF16)

    def head_norm_rope(y, cols):
        ss = jnp.dot((y * y).astype(BF16), bd_ref[...], preferred_element_type=F32)
        yn = y * lax.rsqrt(ss * (1.0 / HEAD_DIM) + RMS_EPS) * hg_ref[:, cols]
        c, s1, s2 = c_ref[...], s1_ref[...], s2_ref[...]
        halves = []
        for hb in range(MXU_COLS // LANES):
            yh = yn[:, hb * LANES:(hb + 1) * LANES]
            up = pltpu.roll(yh, LANES - ROT_DIM // 2, 1)
            dn = pltpu.roll(yh, ROT_DIM // 2, 1)
            halves.append(yh * c + up * s1 + dn * s2)
        return jnp.concatenate(halves, axis=1)

    for start, ntiles, normed, transposed in parts:
        o_ref = next(out_refs)
        ot_ref = next(out_refs) if transposed else None

        @pl.when((j >= start) & (j < start + ntiles))
        def _(o_ref=o_ref, ot_ref=ot_ref, normed=normed):
            h = h_ref[...]
            for pb in range(tn // MXU_COLS):
                cols = slice(pb * MXU_COLS, (pb + 1) * MXU_COLS)
                y = jnp.dot(h, w_ref[:, cols].astype(BF16), preferred_element_type=F32)
                if normed:
                    y = head_norm_rope(y, cols)
                o_ref[:, cols] = y
                if ot_ref is not None:
                    ot_ref[cols, :] = y.T


def _norm_proj(x, g, w, head_gain, rope_tabs, parts, seq=None):
    n, d = x.shape
    m = w.shape[1]
    tm, tn = min(TM_TOKENS, n), TN_PROJ
    assert n % tm == 0 and m % tn == 0 and tn % MXU_COLS == 0 and all(p[0] % tn == 0 for p in parts)
    tile_parts, start = [], 0
    for width, normed, transposed in parts:
        tile_parts.append((start, width // tn, normed, transposed))
        start += width // tn
    c, s1, s2 = rope_tabs
    bd = (jnp.arange(MXU_COLS)[:, None] // HEAD_DIM == jnp.arange(MXU_COLS)[None, :] // HEAD_DIM).astype(BF16)

    out_specs, out_shape = [], []
    for (st, nt, _, transposed), (width, _, _) in zip(tile_parts, parts):
        col = lambda j, st=st, nt=nt: jnp.clip(j - st, 0, nt - 1)
        out_specs.append(pl.BlockSpec((tm, tn), lambda i, j, col=col: (i, col(j))))
        out_shape.append(jax.ShapeDtypeStruct((n, width), F32))
        if transposed:
            assert seq is not None and seq % tm == 0 and n % seq == 0
            per_seq = seq // tm
            out_specs.append(pl.BlockSpec((None, tn, tm),
                                          lambda i, j, col=col: (i // per_seq, col(j), i % per_seq)))
            out_shape.append(jax.ShapeDtypeStruct((n // seq, width, seq), F32))

    tab_spec = pl.BlockSpec((tm, LANES), lambda i, j: (i, 0))
    return pl.pallas_call(
        functools.partial(_norm_proj_kernel, parts=tuple(tile_parts), tn=tn),
        grid=(n // tm, m // tn),
        in_specs=[
            pl.BlockSpec((tm, d), lambda i, j: (i, 0)),
            pl.BlockSpec((1, d), lambda i, j: (0, 0)),
            pl.BlockSpec((d, tn), lambda i, j: (0, j)),
            pl.BlockSpec((1, tn), lambda i, j: (0, j)),
            tab_spec, tab_spec, tab_spec,
            pl.BlockSpec((MXU_COLS, MXU_COLS), lambda i, j: (0, 0)),
        ],
        out_specs=out_specs,
        out_shape=out_shape,
        scratch_shapes=[pltpu.VMEM((tm, d), BF16)],
        compiler_params=_cparams("parallel", "arbitrary"),
        name="norm_proj",
    )(x, g.reshape(1, d), w, head_gain.reshape(1, m), c, s1, s2, bd)


def _rope_tables(pos):
    half = ROT_DIM // 2
    inv = ROPE_THETA ** (-2.0 * jnp.arange(half, dtype=F32) / ROT_DIM)
    ang = pos.astype(F32)[:, None] * inv[None, :]
    cos, sin = jnp.cos(ang), jnp.sin(ang)
    n = pos.shape[0]
    zeros = jnp.zeros((n, HEAD_DIM - ROT_DIM), F32)
    zh = jnp.zeros((n, half), F32)
    c = jnp.concatenate([cos, cos, jnp.ones((n, HEAD_DIM - ROT_DIM), F32)], axis=1)
    s1 = jnp.concatenate([-sin, zh, zeros], axis=1)
    s2 = jnp.concatenate([zh, sin, zeros], axis=1)
    tile = lambda t: jnp.tile(t, (1, HEADS_PER_VREG))
    return tile(c), tile(s1), tile(s2)


def _topk_block_mask(gate_t, n_valid):
    nblk = gate_t.shape[0]
    blk_idx = lax.broadcasted_iota(I32, gate_t.shape, 0)
    rank = jnp.zeros(gate_t.shape, I32)
    for m in range(nblk):
        gm = gate_t[m:m + 1, :]
        beats = (gm > gate_t) | ((gm == gate_t) & (m < blk_idx))
        rank = rank + jnp.where(beats & (m < n_valid), 1, 0)
    return jnp.where((blk_idx < n_valid) & (rank < MOBA_TOPK), 1.0, 0.0)


def _moba_prompt_kernel(q_ref, k_ref, v_ref, o_ref, kb_ref, vt_ref, km_ref, *, nblk, nh_step):
    qi = pl.program_id(2)
    blk = MOBA_BLOCK

    @pl.when(qi == 0)
    def _():
        for n in range(nblk):
            rows = slice(n * blk, (n + 1) * blk)
            half = slice((n % 2) * blk, (n % 2 + 1) * blk)
            kn = k_ref[rows, :]
            kb_ref[n // 2, half, :] = kn.astype(BF16)
            vt_ref[n // 2, :, half] = v_ref[rows, :].T.astype(BF16)
            km_ref[n:n + 1, :] = jnp.mean(kn, axis=0, keepdims=True)

    q_t = q_ref[...].T
    head_of_row = lax.broadcasted_iota(I32, (nh_step * HEAD_DIM, blk), 0) // HEAD_DIM
    key_i = lax.broadcasted_iota(I32, (blk, blk), 0)
    qry_i = lax.broadcasted_iota(I32, (blk, blk), 1)
    blk_idx = lax.broadcasted_iota(I32, (nblk, blk), 0)
    km = km_ref[...]

    qs, sel_t = [], []
    for h in range(nh_step):
        qh = jnp.where(head_of_row == h, q_t, 0.0)
        gate_t = jnp.dot(km, qh, precision=HIGHEST, preferred_element_type=F32)
        sel_t.append(jnp.where(blk_idx == qi, 1.0, _topk_block_mask(gate_t, qi)))
        qs.append((qh * ATTN_SCALE).astype(BF16))

    def block_pair(pi, carry):
        k2 = kb_ref[pi]
        vt2 = vt_ref[pi]
        new = []
        for h in range(nh_step):
            m, l, a = carry[h]
            s = jnp.dot(k2, qs[h], preferred_element_type=F32)
            halves = []
            for c in range(2):
                j = 2 * pi + c
                vis_j = jnp.max(jnp.where(blk_idx == j, sel_t[h], 0.0), axis=0, keepdims=True)
                causal_slack = jnp.where(j == qi, 0, blk)
                ok = (key_i <= qry_i + causal_slack) & (vis_j > 0.0)
                halves.append(jnp.where(ok, s[c * blk:(c + 1) * blk], NEG_INF))
            s = jnp.concatenate(halves, axis=0)
            m_new = jnp.maximum(m, jnp.max(s, axis=0, keepdims=True))
            alpha = jnp.exp(m - m_new)
            p = jnp.exp(s - m_new)
            l = alpha * l + jnp.sum(p, axis=0, keepdims=True)
            dims = slice(h * HEAD_DIM, (h + 1) * HEAD_DIM)
            a = alpha * a + jnp.dot(vt2[dims, :], p.astype(BF16), preferred_element_type=F32)
            new.append((m_new, l, a))
        return tuple(new)

    init = tuple((jnp.full((1, blk), -1e30, F32), jnp.zeros((1, blk), F32), jnp.zeros((HEAD_DIM, blk), F32))
                 for _ in range(nh_step))
    final = lax.fori_loop(0, qi // 2 + 1, block_pair, init)
    o_ref[...] = jnp.concatenate([a / l for _, l, a in final], axis=0).T


MOBA_HEADS_PER_STEP = 4


def _moba_prompt(q, k, v):
    b, s, hd = q.shape
    nblk = s // MOBA_BLOCK
    w = MOBA_HEADS_PER_STEP * HEAD_DIM
    assert s % (2 * MOBA_BLOCK) == 0 and hd % w == 0 and w % LANES == 0
    q_spec = pl.BlockSpec((None, MOBA_BLOCK, w), lambda bi, hp, qi: (bi, qi, hp))
    kv_spec = pl.BlockSpec((None, s, w), lambda bi, hp, qi: (bi, 0, hp))
    return pl.pallas_call(
        functools.partial(_moba_prompt_kernel, nblk=nblk, nh_step=MOBA_HEADS_PER_STEP),
        grid=(b, hd // w, nblk),
        in_specs=[q_spec, kv_spec, kv_spec],
        out_specs=q_spec,
        out_shape=jax.ShapeDtypeStruct((b, s, hd), F32),
        scratch_shapes=[pltpu.VMEM((nblk // 2, 2 * MOBA_BLOCK, w), BF16),
                        pltpu.VMEM((nblk // 2, w, 2 * MOBA_BLOCK), BF16),
                        pltpu.VMEM((nblk, w), F32)],
        compiler_params=_cparams("parallel", "parallel", "arbitrary"),
        name="moba_prompt",
    )(q, k, v)


def _moba_sample_kernel(pt_ref, q_ref, kn_ref, vn_ref, ka_ref, kbp_ref, va_ref, vbp_ref, o_ref,
                        qf_ref, q2_ref, g_ref, m_ref, l_ref, acc_ref, *, nblk, ns, nh):
    del pt_ref
    n = pl.program_id(1)
    rows = ns * nh
    d = nh * HEAD_DIM
    page = MOBA_BLOCK // 2
    contract_last = (((1,), (1,)), ((), ()))

    @pl.when(n == 0)
    def _():
        q = q_ref[...]
        head_of_lane = lax.broadcasted_iota(I32, (nh, d), 1) // HEAD_DIM
        head_of_row = lax.broadcasted_iota(I32, (nh, d), 0)
        for qi in range(ns):
            piece = jnp.where(head_of_lane == head_of_row, jnp.broadcast_to(q[qi:qi + 1, :], (nh, d)), 0.0)
            hi = piece.astype(BF16)
            qf_ref[qi * nh:(qi + 1) * nh, :] = piece
            q2_ref[qi * nh:(qi + 1) * nh, :] = (piece * ATTN_SCALE).astype(BF16)
            q2_ref[rows + qi * nh:rows + (qi + 1) * nh, :] = (piece - hi.astype(F32)).astype(BF16)

    q2 = q2_ref[...]

    def page_scores(k_page_ref):
        k = k_page_ref[...]
        k_hi = k.astype(BF16)
        k_lo = (k - k_hi.astype(F32)).astype(BF16)
        s2 = jnp.dot(q2, k_hi, preferred_element_type=F32)
        s_lo = jnp.dot(q2[:rows], k_lo, preferred_element_type=F32)
        return s2[:rows], (s2[:rows] + s_lo) * (1.0 / ATTN_SCALE) + s2[rows:]

    sa, ga = page_scores(ka_ref)
    sb, gb = page_scores(kbp_ref)
    s = jnp.concatenate([sa, sb], axis=1)
    gate = (jnp.sum(ga, axis=1, keepdims=True) + jnp.sum(gb, axis=1, keepdims=True)) * (1.0 / MOBA_BLOCK)
    m = jnp.max(s, axis=1, keepdims=True)
    p = jnp.exp(s - m)
    l = jnp.sum(p, axis=1, keepdims=True)
    pb = p.astype(BF16)
    g_ref[n] = jnp.broadcast_to(gate, (rows, LANES))
    m_ref[n] = jnp.broadcast_to(m, (rows, LANES))
    l_ref[n] = jnp.broadcast_to(l, (rows, LANES))
    acc_ref[n] = (lax.dot_general(pb[:, :page], va_ref[...].astype(BF16), contract_last, preferred_element_type=F32)
                  + lax.dot_general(pb[:, page:], vbp_ref[...].astype(BF16), contract_last,
                                    preferred_element_type=F32))

    @pl.when(n == nblk - 1)
    def _():
        sel = []
        for a in range(nblk):
            ga = g_ref[a]
            rank = jnp.zeros((rows, LANES), I32)
            for b in range(nblk):
                gb = g_ref[b]
                beats = (gb > ga) if b > a else (gb >= ga)
                if b != a:
                    rank = rank + jnp.where(beats, 1, 0)
            sel.append(rank < MOBA_TOPK)
        qf = qf_ref[...]
        kn, vn = kn_ref[...], vn_ref[...]
        q_of_row = lax.broadcasted_iota(I32, (rows, 1), 0) // nh
        s_own = [jnp.sum(qf * kn[j:j + 1, :], axis=1, keepdims=True) * ATTN_SCALE for j in range(ns)]
        s_own = [jnp.where(j <= q_of_row, s_own[j], NEG_INF) for j in range(ns)]
        m_blk = [jnp.where(sel[a], m_ref[a], NEG_INF)[:, 0:1] for a in range(nblk)]
        m_all = functools.reduce(jnp.maximum, s_own + m_blk)
        den = jnp.zeros((rows, 1), F32)
        num = jnp.zeros((rows, d), F32)
        for j in range(ns):
            w = jnp.exp(s_own[j] - m_all)
            den = den + w
            num = num + w * vn[j:j + 1, :]
        for a in range(nblk):
            w = jnp.exp(m_blk[a] - m_all)
            den = den + w * l_ref[a][:, 0:1]
            num = num + w * acc_ref[a]
        res = num / den
        head_of_lane = lax.broadcasted_iota(I32, (rows, d), 1) // HEAD_DIM
        head_of_row = lax.broadcasted_iota(I32, (rows, d), 0) % nh
        res = jnp.where(head_of_lane == head_of_row, res, 0.0)
        o_ref[...] = jnp.sum(res.reshape(ns, nh, d), axis=1)


def _moba_sample(q, k_new, v_new, pool_k, pool_v, page_table):
    nb, ns, d = q.shape
    n_phys, page, nh, _ = pool_k.shape
    nblk = page_table.shape[1] * page // MOBA_BLOCK
    assert 2 * page == MOBA_BLOCK and page_table.shape[1] * page == nblk * MOBA_BLOCK
    assert ns <= MOBA_BLOCK and nh % 8 == 0 and d == nh * HEAD_DIM
    rows = ns * nh
    pool_k = jnp.transpose(pool_k, (0, 2, 3, 1)).reshape(n_phys, d, page)
    pool_v = jnp.transpose(pool_v, (0, 2, 3, 1)).reshape(n_phys, d, page)
    new_spec = pl.BlockSpec((None, ns, d), lambda b, n, pt: (b, 0, 0))
    page_a = pl.BlockSpec((None, d, page), lambda b, n, pt: (pt[b, 2 * n], 0, 0))
    page_b = pl.BlockSpec((None, d, page), lambda b, n, pt: (pt[b, 2 * n + 1], 0, 0))
    return pl.pallas_call(
        functools.partial(_moba_sample_kernel, nblk=nblk, ns=ns, nh=nh),
        grid_spec=pltpu.PrefetchScalarGridSpec(
            num_scalar_prefetch=1,
            grid=(nb, nblk),
            in_specs=[new_spec, new_spec, new_spec, page_a, page_b, page_a, page_b],
            out_specs=new_spec,
            scratch_shapes=[
                pltpu.VMEM((rows, d), F32), pltpu.VMEM((2 * rows, d), BF16),
                pltpu.VMEM((nblk, rows, LANES), F32), pltpu.VMEM((nblk, rows, LANES), F32),
                pltpu.VMEM((nblk, rows, LANES), F32), pltpu.VMEM((nblk, rows, d), F32),
            ]),
        out_shape=jax.ShapeDtypeStruct((nb, ns, d), F32),
        compiler_params=_cparams("parallel", "arbitrary"),
        name="moba_sample",
    )(page_table, q, k_new, v_new, pool_k, pool_k, pool_v, pool_v)


def _matmul_residual_kernel(a_ref, w_ref, r_ref, o_ref):
    o_ref[...] = r_ref[...] + jnp.dot(a_ref[...].astype(BF16), w_ref[...].astype(BF16),
                                      preferred_element_type=F32)


def _matmul_residual(a, w, res):
    n, k = a.shape
    m = w.shape[1]
    tm = min(TM_TOKENS, n)
    assert n % tm == 0
    return pl.pallas_call(
        _matmul_residual_kernel,
        grid=(n // tm,),
        in_specs=[pl.BlockSpec((tm, k), lambda i: (i, 0)),
                  pl.BlockSpec((k, m), lambda i: (0, 0)),
                  pl.BlockSpec((tm, m), lambda i: (i, 0))],
        out_specs=pl.BlockSpec((tm, m), lambda i: (i, 0)),
        out_shape=jax.ShapeDtypeStruct((n, m), F32),
        compiler_params=_cparams("parallel"),
        name="matmul_residual",
    )(a, w, res)


def _dense_ffn_kernel(x_ref, g_ref, wg_ref, wu_ref, wd_ref, o_ref, h_ref):
    @pl.when(pl.program_id(1) == 0)
    def _():
        x = x_ref[...]
        h_ref[...] = _rmsnorm_rows(x, g_ref[...]).astype(BF16)
        o_ref[...] = x
    h = h_ref[...]
    gate = jnp.dot(h, wg_ref[...].astype(BF16), preferred_element_type=F32)
    up = jnp.dot(h, wu_ref[...].astype(BF16), preferred_element_type=F32)
    act = (gate * jax.nn.sigmoid(gate)) * up
    o_ref[...] += jnp.dot(act.astype(BF16), wd_ref[...].astype(BF16), preferred_element_type=F32)


def _dense_ffn(x, g, w_gu, w_down):
    n, d = x.shape
    f = w_down.shape[0]
    tm, tf = min(TM_FFN, n), TF_FFN
    assert n % tm == 0 and f % tf == 0
    nf = f // tf
    return pl.pallas_call(
        _dense_ffn_kernel,
        grid=(n // tm, nf),
        in_specs=[pl.BlockSpec((tm, d), lambda i, c: (i, 0)),
                  pl.BlockSpec((1, d), lambda i, c: (0, 0)),
                  pl.BlockSpec((d, tf), lambda i, c: (0, c)),
                  pl.BlockSpec((d, tf), lambda i, c: (0, nf + c)),
                  pl.BlockSpec((tf, d), lambda i, c: (c, 0))],
        out_specs=pl.BlockSpec((tm, d), lambda i, c: (i, 0)),
        out_shape=jax.ShapeDtypeStruct((n, d), F32),
        scratch_shapes=[pltpu.VMEM((tm, d), BF16)],
        compiler_params=_cparams("parallel", "arbitrary"),
        name="dense_ffn",
    )(x, g.reshape(1, d), w_gu, w_gu, w_down)


def _merge_groups(outs, lses):
    m = functools.reduce(jnp.maximum, lses)
    ws = [jnp.exp(l - m) for l in lses]
    num = functools.reduce(lambda a, b: a + b, [w * o for w, o in zip(ws, outs)])
    return num / functools.reduce(lambda a, b: a + b, ws)


def _dilated_prompt_kernel(q0_ref, q1_ref, q2_ref, k_ref, v_ref, o_ref,
                           qd_ref, kd_ref, vt_ref, od_ref, ld_ref, og_ref, lg_ref, *, s):
    span = DIL_SPAN
    ntiles = s // span
    head_of_row = lax.broadcasted_iota(I32, (LANES, span), 0) // HEAD_DIM
    key_i = lax.broadcasted_iota(I32, (span, span), 0)
    qry_i = lax.broadcasted_iota(I32, (span, span), 1)

    for g, (q_ref, (_, dil)) in enumerate(zip((q0_ref, q1_ref, q2_ref), DIL_GROUPS)):
        sub = s // dil
        tiles_per_class = sub // span
        for r in range(dil):
            src = pl.ds(r, sub, stride=dil) if dil > 1 else pl.ds(0, s)
            dst = slice(r * sub, (r + 1) * sub)
            qd_ref[dst, :] = q_ref[src, :] * ATTN_SCALE
            kd_ref[dst, :] = k_ref[src, :].astype(BF16)
            vr = v_ref[src, :]
            for tt in range(tiles_per_class):
                vt_ref[r * tiles_per_class + tt] = vr[tt * span:(tt + 1) * span, :].T.astype(BF16)

        def tile(t, carry, tiles_per_class=tiles_per_class):
            t_prev = jnp.maximum(t - 1, 0)
            cur = pl.ds(pl.multiple_of(t * span, span), span)
            prev = pl.ds(pl.multiple_of(t_prev * span, span), span)
            prev_off = jnp.where((t % tiles_per_class) != 0, 0, span)
            q_t = qd_ref[cur, :].T
            k2 = jnp.concatenate([kd_ref[prev, :], kd_ref[cur, :]], axis=0)
            vt2 = jnp.concatenate([vt_ref[t_prev], vt_ref[t]], axis=1)
            outs, lses = [], []
            for h in range(HEADS_PER_VREG):
                qh = jnp.where(head_of_row == h, q_t, 0.0).astype(BF16)
                sa = jnp.dot(k2, qh, preferred_element_type=F32)
                sp = jnp.where(key_i >= qry_i + prev_off, sa[:span], NEG_INF)
                sc = jnp.where(key_i <= qry_i, sa[span:], NEG_INF)
                sa = jnp.concatenate([sp, sc], axis=0)
                m = jnp.max(sa, axis=0, keepdims=True)
                p = jnp.exp(sa - m)
                l = jnp.sum(p, axis=0, keepdims=True)
                dims = slice(h * HEAD_DIM, (h + 1) * HEAD_DIM)
                outs.append(jnp.dot(vt2[dims, :], p.astype(BF16), preferred_element_type=F32) / l)
                lses.append(jnp.broadcast_to(m + jnp.log(l), (HEAD_DIM, span)))
            od_ref[cur, :] = jnp.concatenate(outs, axis=0).T
            ld_ref[cur, :] = jnp.concatenate(lses, axis=0).T
            return carry

        lax.fori_loop(0, ntiles, tile, 0, unroll=4)
        for r in range(dil):
            dst = pl.ds(r, sub, stride=dil) if dil > 1 else pl.ds(0, s)
            src = slice(r * sub, (r + 1) * sub)
            og_ref[g, dst, :] = od_ref[src, :]
            lg_ref[g, dst, :] = ld_ref[src, :]

    ng = len(DIL_GROUPS)
    o_ref[...] = _merge_groups([og_ref[g] for g in range(ng)], [lg_ref[g] for g in range(ng)])


def _dilated_prompt(q, k, v):
    b, s, hd = k.shape
    ng = len(DIL_GROUPS)
    assert q.shape == (b, s, ng * hd) and hd % LANES == 0
    assert all(s % (DIL_SPAN * dil) == 0 and win // dil == DIL_SPAN for win, dil in DIL_GROUPS)
    hp_blocks = hd // LANES

    def q_spec(g):
        return pl.BlockSpec((None, s, LANES), lambda bi, hp: (bi, 0, g * hp_blocks + hp))

    kv_spec = pl.BlockSpec((None, s, LANES), lambda bi, hp: (bi, 0, hp))
    return pl.pallas_call(
        functools.partial(_dilated_prompt_kernel, s=s),
        grid=(b, hp_blocks),
        in_specs=[q_spec(0), q_spec(1), q_spec(2), kv_spec, kv_spec],
        out_specs=kv_spec,
        out_shape=jax.ShapeDtypeStruct((b, s, hd), F32),
        scratch_shapes=[pltpu.VMEM((s, LANES), F32), pltpu.VMEM((s, LANES), BF16),
                        pltpu.VMEM((s // DIL_SPAN, LANES, DIL_SPAN), BF16),
                        pltpu.VMEM((s, LANES), F32), pltpu.VMEM((s, LANES), F32),
                        pltpu.VMEM((ng, s, LANES), F32), pltpu.VMEM((ng, s, LANES), F32)],
        compiler_params=_cparams("parallel", "parallel"),
        name="dilated_prompt",
    )(q, q, q, k, v)


def _dilated_sample_kernel(q_ref, kn_ref, vn_ref, kt_ref, vt_ref, o_ref, *, ns, nh, w_buf):
    d = nh * HEAD_DIM
    ng = len(DIL_GROUPS)
    rows = ng * ns * nh
    contract_last = (((1,), (1,)), ((), ()))

    q = q_ref[...]
    head_of_lane = lax.broadcasted_iota(I32, (nh, d), 1) // HEAD_DIM
    head_of_row = lax.broadcasted_iota(I32, (nh, d), 0)
    pieces = []
    for g in range(ng):
        for qi in range(ns):
            qrow = jnp.broadcast_to(q[qi:qi + 1, g * d:(g + 1) * d], (nh, d))
            pieces.append(jnp.where(head_of_lane == head_of_row, qrow, 0.0))
    qf = jnp.concatenate(pieces, axis=0)
    qb = (qf * ATTN_SCALE).astype(BF16)

    kn, vn = kn_ref[...], vn_ref[...]
    s_new = [jnp.sum(qf * kn[j:j + 1, :], axis=1, keepdims=True) * ATTN_SCALE for j in range(ns)]

    r1 = lax.broadcasted_iota(I32, (rows, 1), 0)
    g1, q1 = r1 // (ns * nh), (r1 // nh) % ns
    win_r = jnp.zeros((rows, 1), I32)
    low_bits_r = jnp.zeros((rows, 1), I32)
    for g, (win, dil) in enumerate(DIL_GROUPS):
        win_r = jnp.where(g1 == g, win, win_r)
        low_bits_r = jnp.where(g1 == g, dil - 1, low_bits_r)

    s = jnp.dot(qb, kt_ref[...].astype(BF16), preferred_element_type=F32)
    dist = (w_buf + q1) - lax.broadcasted_iota(I32, (rows, w_buf), 1)
    s = jnp.where(((dist & low_bits_r) == 0) & (dist <= win_r), s, NEG_INF)
    s_new = [jnp.where((q1 >= j) & (((q1 - j) & low_bits_r) == 0) & (q1 - j <= win_r), s_new[j], NEG_INF)
             for j in range(ns)]
    m = functools.reduce(jnp.maximum, [jnp.max(s, axis=1, keepdims=True)] + s_new)
    p = jnp.exp(s - m)
    l = jnp.sum(p, axis=1, keepdims=True)
    acc = lax.dot_general(p.astype(BF16), vt_ref[...].astype(BF16), contract_last, preferred_element_type=F32)
    for j in range(ns):
        w = jnp.exp(s_new[j] - m)
        l = l + w
        acc = acc + w * vn[j:j + 1, :]
    o_all = acc / l
    lse_all = jnp.broadcast_to(m + jnp.log(l), o_all.shape)

    own = (lax.broadcasted_iota(I32, o_all.shape, 1) // HEAD_DIM) == (lax.broadcasted_iota(I32, o_all.shape, 0) % nh)
    o_gq = jnp.sum(jnp.where(own, o_all, 0.0).reshape(ng * ns, nh, d), axis=1)
    lse_gq = jnp.sum(jnp.where(own, lse_all, 0.0).reshape(ng * ns, nh, d), axis=1)
    o_ref[...] = _merge_groups([o_gq[g * ns:(g + 1) * ns] for g in range(ng)],
                               [lse_gq[g * ns:(g + 1) * ns] for g in range(ng)])


def _dilated_sample(q, k_new, v_new, cache_k, cache_v):
    nb, ns, d = k_new.shape
    _, w_buf, nh, _ = cache_k.shape
    ng = len(DIL_GROUPS)
    assert all(dil & (dil - 1) == 0 for _, dil in DIL_GROUPS)
    assert d == nh * HEAD_DIM and nh % 8 == 0 and w_buf % LANES == 0 and q.shape == (nb, ns, ng * d)
    transposed = lambda c: jnp.transpose(c, (0, 2, 3, 1)).reshape(nb, d, w_buf)
    new_spec = pl.BlockSpec((None, ns, d), lambda b: (b, 0, 0))
    buf_spec = pl.BlockSpec((None, d, w_buf), lambda b: (b, 0, 0))
    return pl.pallas_call(
        functools.partial(_dilated_sample_kernel, ns=ns, nh=nh, w_buf=w_buf),
        grid=(nb,),
        in_specs=[pl.BlockSpec((None, ns, ng * d), lambda b: (b, 0, 0)), new_spec, new_spec, buf_spec, buf_spec],
        out_specs=new_spec,
        out_shape=jax.ShapeDtypeStruct((nb, ns, d), F32),
        compiler_params=_cparams("parallel"),
        name="dilated_sample",
    )(q, k_new, v_new, transposed(cache_k), transposed(cache_v))


def _router_kernel(x_ref, g_ref, wr_ref, h_ref, route_ref):
    h = _rmsnorm_rows(x_ref[...], g_ref[...])
    h_ref[...] = h
    logits = jnp.dot(h, wr_ref[...], precision=HIGHEST, preferred_element_type=F32)
    lane = lax.broadcasted_iota(I32, logits.shape, 1)
    lane_f = lane.astype(F32)
    l1 = jnp.where(lane < N_EXPERTS, logits, NEG_INF)
    m1 = jnp.max(l1, axis=1, keepdims=True)
    i1 = jnp.min(jnp.where(l1 == m1, lane_f, float(LANES)), axis=1, keepdims=True)
    l2 = jnp.where(lane_f == i1, NEG_INF, l1)
    m2 = jnp.max(l2, axis=1, keepdims=True)
    i2 = jnp.min(jnp.where(l2 == m2, lane_f, float(LANES)), axis=1, keepdims=True)
    e = jnp.exp(m2 - m1)
    den = 1.0 + e
    packed = jnp.where(lane == 0, i1, jnp.where(lane == 1, i2,
             jnp.where(lane == 2, 1.0 / den, jnp.where(lane == 3, e / den, 0.0))))
    route_ref[...] = packed[:, :route_ref.shape[1]]


ROUTE_COLS = 8


def _router(x, g, w_router_padded):
    n, d = x.shape
    tm = min(TM_ROUTE, n)
    assert n % tm == 0
    return pl.pallas_call(
        _router_kernel,
        grid=(n // tm,),
        in_specs=[pl.BlockSpec((tm, d), lambda i: (i, 0)),
                  pl.BlockSpec((1, d), lambda i: (0, 0)),
                  pl.BlockSpec((d, LANES), lambda i: (0, 0))],
        out_specs=[pl.BlockSpec((tm, d), lambda i: (i, 0)),
                   pl.BlockSpec((tm, ROUTE_COLS), lambda i: (i, 0))],
        out_shape=[jax.ShapeDtypeStruct((n, d), F32), jax.ShapeDtypeStruct((n, ROUTE_COLS), F32)],
        compiler_params=_cparams("parallel"),
        name="moe_router",
    )(x, g.reshape(1, d), w_router_padded)


def _dispatch_kernel(dest_ref, h_ref, slots_in, slots_out, sem, *, tm):
    del slots_in

    def row_copy(r, k):
        return pltpu.make_async_copy(h_ref.at[pl.ds(r, 1)],
                                     slots_out.at[pl.ds(dest_ref[0, TOP_K * r + k], 1)], sem)

    def issue(r, c):
        for k in range(TOP_K):
            row_copy(r, k).start()
        return c

    def drain(r, c):
        for k in range(TOP_K):
            row_copy(r, k).wait()
        return c

    lax.fori_loop(0, tm, issue, 0, unroll=8)
    lax.fori_loop(0, tm, drain, 0, unroll=8)


def _dispatch(h, dest, slots):
    n, d = h.shape
    tm = min(TM_ROUTE, n)
    assert n % tm == 0
    return pl.pallas_call(
        functools.partial(_dispatch_kernel, tm=tm),
        grid=(n // tm,),
        in_specs=[pl.BlockSpec((None, 1, TOP_K * tm), lambda i: (i, 0, 0), memory_space=pltpu.SMEM),
                  pl.BlockSpec((tm, d), lambda i: (i, 0)),
                  pl.BlockSpec(memory_space=pl.ANY)],
        out_specs=pl.BlockSpec(memory_space=pl.ANY),
        out_shape=jax.ShapeDtypeStruct(slots.shape, slots.dtype),
        scratch_shapes=[pltpu.SemaphoreType.DMA(())],
        input_output_aliases={2: 0},
        compiler_params=_cparams("arbitrary"),
        name="moe_dispatch",
    )(dest.reshape(n // tm, 1, TOP_K * tm), h, slots)


def _moe_gu_kernel(exp_ref, valid_ref, first_ref, x_ref, wg_ref, wu_ref, o_ref, wgb_ref, wub_ref):
    del exp_ref
    t = pl.program_id(1)

    @pl.when(first_ref[t] == 1)
    def _():
        wgb_ref[...] = wg_ref[...].astype(BF16)
        wub_ref[...] = wu_ref[...].astype(BF16)

    @pl.when(valid_ref[t] == 1)
    def _():
        x = x_ref[...].astype(BF16)
        gate = jnp.dot(x, wgb_ref[...], preferred_element_type=F32)
        up = jnp.dot(x, wub_ref[...], preferred_element_type=F32)
        o_ref[...] = ((gate * jax.nn.sigmoid(gate)) * up).astype(BF16)

    @pl.when(valid_ref[t] == 0)
    def _():
        o_ref[...] = jnp.zeros(o_ref.shape, o_ref.dtype)


def _moe_gu(slots, w_gu, tile_exp, tile_valid, tile_first):
    r, d = slots.shape
    f = w_gu.shape[2] // 2
    tm, tn = TM_MOE, TN_MOE_GU
    assert r % tm == 0 and f % tn == 0
    nj = f // tn
    return pl.pallas_call(
        _moe_gu_kernel,
        grid_spec=pltpu.PrefetchScalarGridSpec(
            num_scalar_prefetch=3,
            grid=(nj, r // tm),
            in_specs=[pl.BlockSpec((tm, d), lambda j, t, ex, va, fi: (t, 0)),
                      pl.BlockSpec((None, d, tn), lambda j, t, ex, va, fi: (ex[t], 0, j)),
                      pl.BlockSpec((None, d, tn), lambda j, t, ex, va, fi: (ex[t], 0, nj + j))],
            out_specs=pl.BlockSpec((tm, tn), lambda j, t, ex, va, fi: (t, j)),
            scratch_shapes=[pltpu.VMEM((d, tn), BF16), pltpu.VMEM((d, tn), BF16)]),
        out_shape=jax.ShapeDtypeStruct((r, f), BF16),
        compiler_params=_cparams("arbitrary", "arbitrary"),
        name="moe_gate_up",
    )(tile_exp, tile_valid, tile_first, slots, w_gu, w_gu)


def _moe_down_kernel(exp_ref, valid_ref, first_ref, a_ref, wd_ref, o_ref, wdb_ref):
    del exp_ref
    t = pl.program_id(1)

    @pl.when(first_ref[t] == 1)
    def _():
        wdb_ref[...] = wd_ref[...].astype(BF16)

    @pl.when(valid_ref[t] == 1)
    def _():
        o_ref[...] = jnp.dot(a_ref[...], wdb_ref[...], preferred_element_type=F32)

    @pl.when(valid_ref[t] == 0)
    def _():
        o_ref[...] = jnp.zeros(o_ref.shape, o_ref.dtype)


def _moe_down(act, w_down, tile_exp, tile_valid, tile_first):
    r, f = act.shape
    d = w_down.shape[2]
    tm, tn = TM_MOE, TN_MOE_DOWN
    assert r % tm == 0 and d % tn == 0
    return pl.pallas_call(
        _moe_down_kernel,
        grid_spec=pltpu.PrefetchScalarGridSpec(
            num_scalar_prefetch=3,
            grid=(d // tn, r // tm),
            in_specs=[pl.BlockSpec((tm, f), lambda j, t, ex, va, fi: (t, 0)),
                      pl.BlockSpec((None, f, tn), lambda j, t, ex, va, fi: (ex[t], 0, j))],
            out_specs=pl.BlockSpec((tm, tn), lambda j, t, ex, va, fi: (t, j)),
            scratch_shapes=[pltpu.VMEM((f, tn), BF16)]),
        out_shape=jax.ShapeDtypeStruct((r, d), F32),
        compiler_params=_cparams("arbitrary", "arbitrary"),
        name="moe_down",
    )(tile_exp, tile_valid, tile_first, act, w_down)


def _combine_kernel(dest_ref, x_ref, route_ref, rows_hbm, y_ref, buf_ref, sem, *, tm):
    def row_copy(r, k):
        return pltpu.make_async_copy(rows_hbm.at[pl.ds(dest_ref[0, TOP_K * r + k], 1)],
                                     buf_ref.at[k, pl.ds(r, 1)], sem)

    def issue(r, c):
        for k in range(TOP_K):
            row_copy(r, k).start()
        return c

    def drain(r, c):
        for k in range(TOP_K):
            row_copy(r, k).wait()
        return c

    lax.fori_loop(0, tm, issue, 0, unroll=8)
    lax.fori_loop(0, tm, drain, 0, unroll=8)
    route = route_ref[...]
    y_ref[...] = x_ref[...] + (route[:, 2:3] * buf_ref[0] + route[:, 3:4] * buf_ref[1])


def _combine(x, route, dest, rows):
    n, d = x.shape
    tm = min(TM_ROUTE, n)
    assert n % tm == 0
    return pl.pallas_call(
        functools.partial(_combine_kernel, tm=tm),
        grid=(n // tm,),
        in_specs=[pl.BlockSpec((None, 1, TOP_K * tm), lambda i: (i, 0, 0), memory_space=pltpu.SMEM),
                  pl.BlockSpec((tm, d), lambda i: (i, 0)),
                  pl.BlockSpec((tm, ROUTE_COLS), lambda i: (i, 0)),
                  pl.BlockSpec(memory_space=pl.ANY)],
        out_specs=pl.BlockSpec((tm, d), lambda i: (i, 0)),
        out_shape=jax.ShapeDtypeStruct((n, d), F32),
        scratch_shapes=[pltpu.VMEM((TOP_K, tm, d), F32), pltpu.SemaphoreType.DMA(())],
        compiler_params=_cparams("arbitrary"),
        name="moe_combine",
    )(dest.reshape(n // tm, 1, TOP_K * tm), x, route, rows)


def _moe_ffn(xs, g, w_router, w_gu, w_down):
    d = xs[0].shape[1]
    wr = jnp.zeros((d, LANES), F32).at[:, :N_EXPERTS].set(w_router)
    routed = [_router(x, g, wr) for x in xs]
    e_flat = jnp.concatenate([route[:, :TOP_K] for _, route in routed], axis=0).astype(I32).reshape(-1)
    a = e_flat.shape[0]
    onehot = (e_flat[:, None] == jnp.arange(N_EXPERTS, dtype=I32)[None, :]).astype(I32)
    csum = jnp.cumsum(onehot, axis=0)
    rank = jnp.sum(csum * onehot, axis=1) - 1
    counts = csum[-1]
    padded = (counts + TM_MOE - 1) // TM_MOE * TM_MOE
    pends = jnp.cumsum(padded)
    dest = (jnp.sum((pends - padded)[None, :] * onehot, axis=1) + rank).astype(I32)
    n_tiles = -(-a // TM_MOE) + N_EXPERTS
    tile_start = jnp.arange(n_tiles, dtype=I32) * TM_MOE
    tile_exp = jnp.minimum(jnp.sum(tile_start[:, None] >= pends[None, :], axis=1), N_EXPERTS - 1).astype(I32)
    tile_valid = (tile_start < pends[-1]).astype(I32)
    tile_first = jnp.concatenate([jnp.ones((1,), I32), (tile_exp[1:] != tile_exp[:-1]).astype(I32)])

    slots = jnp.zeros((n_tiles * TM_MOE, d), F32)
    dests, off = [], 0
    for h, _ in routed:
        dests.append(dest[off:off + TOP_K * h.shape[0]])
        slots = _dispatch(h, dests[-1], slots)
        off += TOP_K * h.shape[0]
    act = _moe_gu(slots, w_gu, tile_exp, tile_valid, tile_first)
    rows = _moe_down(act, w_down, tile_exp, tile_valid, tile_first)
    return [_combine(x, route, dst, rows) for x, (_, route), dst in zip(xs, routed, dests)]


def kernel(x_prompt, x_sample, cache_a_k, cache_a_v, cache_b_k, cache_b_v, page_table, norm_attn_a, w_qkv_a,
           qnorm_a, knorm_a, w_o_a, norm_kv_b, w_kv_b, knorm_b, norm_attn_b, w_q_b, qnorm_b, w_o_b, norm_ffn,
           w_gu_dense, w_down_dense, w_router, w_gu_moe, w_down_moe):
    b, s, d = x_prompt.shape
    nb, ns, _ = x_sample.shape
    depth, n_a = norm_ffn.shape[0], w_qkv_a.shape[0]
    n_phys, page = cache_a_k.shape[1], cache_a_k.shape[2]
    past = page_table.shape[1] * page
    heads_a = d // HEAD_DIM
    d_b = w_kv_b.shape[1] // 2
    heads_b = d_b // HEAD_DIM
    n_groups = len(DIL_GROUPS)
    w_buf = cache_b_k.shape[1]
    w_max = max(w for w, _ in DIL_GROUPS)

    xp = x_prompt.reshape(b * s, d)
    xs = x_sample.reshape(nb * ns, d)
    tabs_p = _rope_tables(jnp.tile(jnp.arange(s), b))
    tabs_s = _rope_tables(jnp.tile(past + jnp.arange(ns), nb))
    ones_d = jnp.ones((d,), F32)
    ak_p, av_p, ak_s, av_s = [], [], [], []
    for l in range(depth):
        if l < n_a:
            gains = jnp.concatenate([jnp.tile(qnorm_a[l], heads_a), jnp.tile(knorm_a[l], heads_a), ones_d])
            qp, kp, kp_t, vp, vp_t = _norm_proj(xp, norm_attn_a[l], w_qkv_a[l], gains, tabs_p,
                                                ((d, True, False), (d, True, True), (d, False, True)), seq=s)
            qs, ks, vs = _norm_proj(xs, norm_attn_a[l], w_qkv_a[l], gains, tabs_s,
                                    ((d, True, False), (d, True, False), (d, False, False)))
            op = _moba_prompt(qp.reshape(b, s, d), kp.reshape(b, s, d), vp.reshape(b, s, d))
            os_ = _moba_sample(qs.reshape(nb, ns, d), ks.reshape(nb, ns, d), vs.reshape(nb, ns, d),
                               cache_a_k[l], cache_a_v[l], page_table)
            xp = _matmul_residual(op.reshape(b * s, d), w_o_a[l], xp)
            xs = _matmul_residual(os_.reshape(nb * ns, d), w_o_a[l], xs)
            ak_p.append(kp_t)
            av_p.append(vp_t)
            ak_s.append(ks)
            av_s.append(vs)
        else:
            j = l - n_a
            gains = jnp.tile(qnorm_b[j], n_groups * heads_b)
            parts = ((n_groups * d_b, True, False),)
            (qp,) = _norm_proj(xp, norm_attn_b[j], w_q_b[j], gains, tabs_p, parts)
            (qs,) = _norm_proj(xs, norm_attn_b[j], w_q_b[j], gains, tabs_s, parts)
            op = _dilated_prompt(qp.reshape(b, s, n_groups * d_b), kb_p.reshape(b, s, d_b), vb_p.reshape(b, s, d_b))
            os_ = _dilated_sample(qs.reshape(nb, ns, n_groups * d_b), kb_s.reshape(nb, ns, d_b),
                                  vb_s.reshape(nb, ns, d_b), cache_b_k, cache_b_v)
            xp = _matmul_residual(op.reshape(b * s, d_b), w_o_b[j], xp)
            xs = _matmul_residual(os_.reshape(nb * ns, d_b), w_o_b[j], xs)
        if l % 2 == 0:
            xp = _dense_ffn(xp, norm_ffn[l], w_gu_dense[l // 2], w_down_dense[l // 2])
            xs = _dense_ffn(xs, norm_ffn[l], w_gu_dense[l // 2], w_down_dense[l // 2])
        else:
            xp, xs = _moe_ffn([xp, xs], norm_ffn[l], w_router[l // 2], w_gu_moe[l // 2], w_down_moe[l // 2])
        if l == n_a - 1:
            gains = jnp.concatenate([jnp.tile(knorm_b, heads_b), jnp.ones((d_b,), F32)])
            kb_p, kb_pt, vb_p, vb_pt = _norm_proj(xp, norm_kv_b, w_kv_b, gains, tabs_p,
                                                  ((d_b, True, True), (d_b, False, True)), seq=s)
            kb_s, vb_s = _norm_proj(xs, norm_kv_b, w_kv_b, gains, tabs_s, ((d_b, True, False), (d_b, False, False)))

    keep = min(w_max, s)
    heads = lambda t, n, h: t.reshape(n_a, *n, h, HEAD_DIM)
    seq_major = lambda t, h: jnp.moveaxis(t.reshape(*t.shape[:-2], h, HEAD_DIM, t.shape[-1]), -1, -3)
    return (xp.reshape(b, s, d), xs.reshape(nb, ns, d),
            seq_major(jnp.stack(ak_p), heads_a), seq_major(jnp.stack(av_p), heads_a),
            seq_major(kb_pt, heads_b)[:, s - keep:], seq_major(vb_pt, heads_b)[:, s - keep:],
            heads(jnp.stack(ak_s), (nb, ns), heads_a), heads(jnp.stack(av_s), (nb, ns), heads_a),
            kb_s.reshape(nb, ns, heads_b, HEAD_DIM), vb_s.reshape(nb, ns, heads_b, HEAD_DIM))
```

```python
import functools

import jax
import jax.numpy as jnp
from jax import lax
from jax.experimental import pallas as pl
from jax.experimental.pallas import tpu as pltpu

F32 = jnp.float32
BF16 = jnp.bfloat16
I32 = jnp.int32
HIGHEST = lax.Precision.HIGHEST
NEG_INF = float("-inf")

HEAD_DIM = 64
MOBA_BLOCK = 256
MOBA_TOPK = 3
DIL_GROUPS = ((128, 1), (512, 4), (2048, 16))
DIL_SPAN = 128
ROT_DIM = HEAD_DIM // 4
ROPE_THETA = 500000.0
ATTN_SCALE = HEAD_DIM ** -0.5
RMS_EPS = 1e-6
N_EXPERTS = 8
TOP_K = 2

LANES = 128
HEADS_PER_VREG = LANES // HEAD_DIM
VMEM_LIMIT_BYTES = 56 * 1024 * 1024

TM_TOKENS = 512
TM_PROJ = 1024
TN_PROJ = 512
MOBA_SAMPLE_BLOCKS_PER_STEP = 4
TM_FFN = 1024
TF_FFN = 256
TM_MOE = 512
TN_MOE_GU = 1792
TN_MOE_DOWN = 512
TM_ROUTE = 512


def _cparams(*sem):
    return pltpu.CompilerParams(dimension_semantics=sem, vmem_limit_bytes=VMEM_LIMIT_BYTES)


def _rmsnorm_rows(x, g):
    ms = jnp.mean(x * x, axis=-1, keepdims=True)
    return x * lax.rsqrt(ms + RMS_EPS) * g


MXU_COLS = 256


def _norm_proj_kernel(x_ref, g_ref, w_ref, hg_ref, c_ref, s1_ref, s2_ref, bd_ref, *rest, parts, tn):
    h_ref = rest[-1]
    out_refs = iter(rest[:-1])
    j = pl.program_id(1)

    @pl.when(j == 0)
    def _():
        h_ref[...] = _rmsnorm_rows(x_ref[...], g_ref[...]).astype(BF16)

    def head_norm_rope(y, cols):
        ss = jnp.dot((y * y).astype(BF16), bd_ref[...], preferred_element_type=F32)
        yn = y * lax.rsqrt(ss * (1.0 / HEAD_DIM) + RMS_EPS) * hg_ref[:, cols]
        c, s1, s2 = c_ref[...], s1_ref[...], s2_ref[...]
        halves = []
        for hb in range(MXU_COLS // LANES):
            yh = yn[:, hb * LANES:(hb + 1) * LANES]
            up = pltpu.roll(yh, LANES - ROT_DIM // 2, 1)
            dn = pltpu.roll(yh, ROT_DIM // 2, 1)
            halves.append(yh * c + up * s1 + dn * s2)
        return jnp.concatenate(halves, axis=1)

    for start, ntiles, normed, transposed in parts:
        o_ref = next(out_refs)
        ot_ref = next(out_refs) if transposed else None

        @pl.when((j >= start) & (j < start + ntiles))
        def _(o_ref=o_ref, ot_ref=ot_ref, normed=normed):
            h = h_ref[...]
            for pb in range(tn // MXU_COLS):
                cols = slice(pb * MXU_COLS, (pb + 1) * MXU_COLS)
                y = jnp.dot(h, w_ref[:, cols].astype(BF16), preferred_element_type=F32)
                if normed:
                    y = head_norm_rope(y, cols)
                o_ref[:, cols] = y
                if ot_ref is not None:
                    ot_ref[cols, :] = y.T


def _norm_proj(x, g, w, head_gain, rope_tabs, parts, seq=None):
    n, d = x.shape
    m = w.shape[1]
    tm, tn = min(TM_PROJ, n), TN_PROJ
    assert n % tm == 0 and m % tn == 0 and tn % MXU_COLS == 0 and all(p[0] % tn == 0 for p in parts)
    tile_parts, start = [], 0
    for width, normed, transposed in parts:
        tile_parts.append((start, width // tn, normed, transposed))
        start += width // tn
    c, s1, s2 = rope_tabs
    bd = (jnp.arange(MXU_COLS)[:, None] // HEAD_DIM == jnp.arange(MXU_COLS)[None, :] // HEAD_DIM).astype(BF16)

    out_specs, out_shape = [], []
    for (st, nt, _, transposed), (width, _, _) in zip(tile_parts, parts):
        col = lambda j, st=st, nt=nt: jnp.clip(j - st, 0, nt - 1)
        out_specs.append(pl.BlockSpec((tm, tn), lambda i, j, col=col: (i, col(j))))
        out_shape.append(jax.ShapeDtypeStruct((n, width), F32))
        if transposed:
            assert seq is not None and seq % tm == 0 and n % seq == 0
            per_seq = seq // tm
            out_specs.append(pl.BlockSpec((None, tn, tm),
                                          lambda i, j, col=col: (i // per_seq, col(j), i % per_seq)))
            out_shape.append(jax.ShapeDtypeStruct((n // seq, width, seq), F32))

    tab_spec = pl.BlockSpec((tm, LANES), lambda i, j: (i, 0))
    return pl.pallas_call(
        functools.partial(_norm_proj_kernel, parts=tuple(tile_parts), tn=tn),
        grid=(n // tm, m // tn),
        in_specs=[
            pl.BlockSpec((tm, d), lambda i, j: (i, 0)),
            pl.BlockSpec((1, d), lambda i, j: (0, 0)),
            pl.BlockSpec((d, tn), lambda i, j: (0, j)),
            pl.BlockSpec((1, tn), lambda i, j: (0, j)),
            tab_spec, tab_spec, tab_spec,
            pl.BlockSpec((MXU_COLS, MXU_COLS), lambda i, j: (0, 0)),
        ],
        out_specs=out_specs,
        out_shape=out_shape,
        scratch_shapes=[pltpu.VMEM((tm, d), BF16)],
        compiler_params=_cparams("parallel", "arbitrary"),
        name="norm_proj",
    )(x, g.reshape(1, d), w, head_gain.reshape(1, m), c, s1, s2, bd)


def _rope_tables(pos):
    half = ROT_DIM // 2
    inv = ROPE_THETA ** (-2.0 * jnp.arange(half, dtype=F32) / ROT_DIM)
    ang = pos.astype(F32)[:, None] * inv[None, :]
    cos, sin = jnp.cos(ang), jnp.sin(ang)
    n = pos.shape[0]
    zeros = jnp.zeros((n, HEAD_DIM - ROT_DIM), F32)
    zh = jnp.zeros((n, half), F32)
    c = jnp.concatenate([cos, cos, jnp.ones((n, HEAD_DIM - ROT_DIM), F32)], axis=1)
    s1 = jnp.concatenate([-sin, zh, zeros], axis=1)
    s2 = jnp.concatenate([zh, sin, zeros], axis=1)
    tile = lambda t: jnp.tile(t, (1, HEADS_PER_VREG))
    return tile(c), tile(s1), tile(s2)


def _topk_block_mask(gate_t, n_valid):
    nblk = gate_t.shape[0]
    blk_idx = lax.broadcasted_iota(I32, gate_t.shape, 0)
    rank = jnp.zeros(gate_t.shape, I32)
    for m in range(nblk):
        gm = gate_t[m:m + 1, :]
        beats = (gm > gate_t) | ((gm == gate_t) & (m < blk_idx))
        rank = rank + jnp.where(beats & (m < n_valid), 1, 0)
    return jnp.where((blk_idx < n_valid) & (rank < MOBA_TOPK), 1.0, 0.0)


def _moba_prompt_kernel(q_ref, k_ref, v_ref, o_ref, kb_ref, vt_ref, vt1_ref, km_ref, *, nblk, nh_step):
    qi = pl.program_id(2)
    blk = MOBA_BLOCK

    @pl.when(qi == 0)
    def _():
        for n in range(nblk):
            rows = slice(n * blk, (n + 1) * blk)
            half = slice((n % 2) * blk, (n % 2 + 1) * blk)
            kn = k_ref[rows, :]
            kb_ref[n // 2, half, :] = kn.astype(BF16)
            vn_t = v_ref[rows, :].T.astype(BF16)
            vt_ref[n // 2, :, half] = vn_t
            vt1_ref[n] = vn_t
            km_ref[n:n + 1, :] = jnp.mean(kn, axis=0, keepdims=True)

    q_t = q_ref[...].T
    head_of_row = lax.broadcasted_iota(I32, (nh_step * HEAD_DIM, blk), 0) // HEAD_DIM
    causal = lax.broadcasted_iota(I32, (blk, blk), 0) <= lax.broadcasted_iota(I32, (blk, blk), 1)
    blk_idx = lax.broadcasted_iota(I32, (nblk, blk), 0)
    km = km_ref[...]
    k_own = kb_ref[qi // 2, pl.ds(pl.multiple_of((qi % 2) * blk, blk), blk), :]
    vt_own = vt1_ref[qi]

    qs, sel_t, init = [], [], []
    for h in range(nh_step):
        qh = jnp.where(head_of_row == h, q_t, 0.0)
        gate_t = jnp.dot(km, qh, precision=HIGHEST, preferred_element_type=F32)
        sel_t.append(_topk_block_mask(gate_t, qi))
        qs.append((qh * ATTN_SCALE).astype(BF16))
        s = jnp.where(causal, jnp.dot(k_own, qs[h], preferred_element_type=F32), NEG_INF)
        m = jnp.max(s, axis=0, keepdims=True)
        p = jnp.exp(s - m)
        dims = slice(h * HEAD_DIM, (h + 1) * HEAD_DIM)
        init.append((m, jnp.sum(p, axis=0, keepdims=True),
                     jnp.dot(vt_own[dims, :], p.astype(BF16), preferred_element_type=F32)))

    def block_pair(pi, carry):
        k2 = kb_ref[pi]
        vt2 = vt_ref[pi]
        new = []
        for h in range(nh_step):
            m, l, a = carry[h]
            s = jnp.dot(k2, qs[h], preferred_element_type=F32)
            vis = [jnp.max(jnp.where(blk_idx == 2 * pi + c, sel_t[h], 0.0), axis=0, keepdims=True) > 0.0
                   for c in range(2)]
            col_max = [jnp.max(s[c * blk:(c + 1) * blk], axis=0, keepdims=True) for c in range(2)]
            m_new = functools.reduce(jnp.maximum, [m] + [jnp.where(v, cm, NEG_INF) for v, cm in zip(vis, col_max)])
            alpha = jnp.exp(m - m_new)
            p = jnp.concatenate([jnp.exp(s[c * blk:(c + 1) * blk] - jnp.where(vis[c], m_new, jnp.inf))
                                 for c in range(2)], axis=0)
            l = alpha * l + jnp.sum(p, axis=0, keepdims=True)
            dims = slice(h * HEAD_DIM, (h + 1) * HEAD_DIM)
            a = alpha * a + jnp.dot(vt2[dims, :], p.astype(BF16), preferred_element_type=F32)
            new.append((m_new, l, a))
        return tuple(new)

    final = lax.fori_loop(0, (qi + 1) // 2, block_pair, tuple(init))
    o_ref[...] = jnp.concatenate([a / l for _, l, a in final], axis=0).T


MOBA_HEADS_PER_STEP = 4


def _moba_prompt(q, k, v):
    b, s, hd = q.shape
    nblk = s // MOBA_BLOCK
    w = MOBA_HEADS_PER_STEP * HEAD_DIM
    assert s % (2 * MOBA_BLOCK) == 0 and hd % w == 0 and w % LANES == 0
    q_spec = pl.BlockSpec((None, MOBA_BLOCK, w), lambda bi, hp, qi: (bi, qi, hp))
    kv_spec = pl.BlockSpec((None, s, w), lambda bi, hp, qi: (bi, 0, hp))
    return pl.pallas_call(
        functools.partial(_moba_prompt_kernel, nblk=nblk, nh_step=MOBA_HEADS_PER_STEP),
        grid=(b, hd // w, nblk),
        in_specs=[q_spec, kv_spec, kv_spec],
        out_specs=q_spec,
        out_shape=jax.ShapeDtypeStruct((b, s, hd), F32),
        scratch_shapes=[pltpu.VMEM((nblk // 2, 2 * MOBA_BLOCK, w), BF16),
                        pltpu.VMEM((nblk // 2, w, 2 * MOBA_BLOCK), BF16),
                        pltpu.VMEM((nblk, w, MOBA_BLOCK), BF16),
                        pltpu.VMEM((nblk, w), F32)],
        compiler_params=_cparams("parallel", "parallel", "arbitrary"),
        name="moba_prompt",
    )(q, k, v)


def _moba_sample_kernel(pt_ref, q_ref, kn_ref, vn_ref, *rest, nblk, ns, nh, blocks_per_step):
    del pt_ref
    n_pages = 2 * blocks_per_step
    k_pages, v_pages = rest[:n_pages], rest[n_pages:2 * n_pages]
    o_ref, qf_ref, q2_ref, g_ref, m_ref, l_ref, acc_ref = rest[2 * n_pages:]
    step = pl.program_id(1)
    n = step * blocks_per_step
    rows = ns * nh
    d = nh * HEAD_DIM
    page = MOBA_BLOCK // 2
    contract_last = (((1,), (1,)), ((), ()))

    @pl.when(step == 0)
    def _():
        q = q_ref[...]
        head_of_lane = lax.broadcasted_iota(I32, (nh, d), 1) // HEAD_DIM
        head_of_row = lax.broadcasted_iota(I32, (nh, d), 0)
        for qi in range(ns):
            piece = jnp.where(head_of_lane == head_of_row, jnp.broadcast_to(q[qi:qi + 1, :], (nh, d)), 0.0)
            hi = piece.astype(BF16)
            qf_ref[qi * nh:(qi + 1) * nh, :] = piece
            q2_ref[qi * nh:(qi + 1) * nh, :] = (piece * ATTN_SCALE).astype(BF16)
            q2_ref[rows + qi * nh:rows + (qi + 1) * nh, :] = (piece - hi.astype(F32)).astype(BF16)

    q2 = q2_ref[...]

    def page_scores(k_page_ref):
        k = k_page_ref[...]
        k_hi = k.astype(BF16)
        k_lo = (k - k_hi.astype(F32)).astype(BF16)
        s2 = jnp.dot(q2, k_hi, preferred_element_type=F32)
        s_lo = jnp.dot(q2[:rows], k_lo, preferred_element_type=F32)
        return s2[:rows], (s2[:rows] + s_lo) * (1.0 / ATTN_SCALE) + s2[rows:]

    for c in range(blocks_per_step):
        sa, ga = page_scores(k_pages[2 * c])
        sb, gb = page_scores(k_pages[2 * c + 1])
        s = jnp.concatenate([sa, sb], axis=1)
        gate = (jnp.sum(ga, axis=1, keepdims=True) + jnp.sum(gb, axis=1, keepdims=True)) * (1.0 / MOBA_BLOCK)
        m = jnp.max(s, axis=1, keepdims=True)
        p = jnp.exp(s - m)
        l = jnp.sum(p, axis=1, keepdims=True)
        pb = p.astype(BF16)
        g_ref[n + c] = jnp.broadcast_to(gate, (rows, LANES))
        m_ref[n + c] = jnp.broadcast_to(m, (rows, LANES))
        l_ref[n + c] = jnp.broadcast_to(l, (rows, LANES))
        acc_ref[n + c] = (
            lax.dot_general(pb[:, :page], v_pages[2 * c][...].astype(BF16), contract_last, preferred_element_type=F32)
            + lax.dot_general(pb[:, page:], v_pages[2 * c + 1][...].astype(BF16), contract_last,
                              preferred_element_type=F32))

    @pl.when(step == nblk // blocks_per_step - 1)
    def _():
        sel = []
        for a in range(nblk):
            ga = g_ref[a]
            rank = jnp.zeros((rows, LANES), I32)
            for b in range(nblk):
                gb = g_ref[b]
                beats = (gb > ga) if b > a else (gb >= ga)
                if b != a:
                    rank = rank + jnp.where(beats, 1, 0)
            sel.append(rank < MOBA_TOPK)
        qf = qf_ref[...]
        kn, vn = kn_ref[...], vn_ref[...]
        q_of_row = lax.broadcasted_iota(I32, (rows, 1), 0) // nh
        s_own = [jnp.sum(qf * kn[j:j + 1, :], axis=1, keepdims=True) * ATTN_SCALE for j in range(ns)]
        s_own = [jnp.where(j <= q_of_row, s_own[j], NEG_INF) for j in range(ns)]
        m_blk = [jnp.where(sel[a], m_ref[a], NEG_INF)[:, 0:1] for a in range(nblk)]
        m_all = functools.reduce(jnp.maximum, s_own + m_blk)
        den = jnp.zeros((rows, 1), F32)
        num = jnp.zeros((rows, d), F32)
        for j in range(ns):
            w = jnp.exp(s_own[j] - m_all)
            den = den + w
            num = num + w * vn[j:j + 1, :]
        for a in range(nblk):
            w = jnp.exp(m_blk[a] - m_all)
            den = den + w * l_ref[a][:, 0:1]
            num = num + w * acc_ref[a]
        res = num / den
        head_of_lane = lax.broadcasted_iota(I32, (rows, d), 1) // HEAD_DIM
        head_of_row = lax.broadcasted_iota(I32, (rows, d), 0) % nh
        res = jnp.where(head_of_lane == head_of_row, res, 0.0)
        o_ref[...] = jnp.sum(res.reshape(ns, nh, d), axis=1)


def _moba_sample(q, k_new, v_new, pool_k, pool_v, page_table):
    nb, ns, d = q.shape
    n_phys, page, nh, _ = pool_k.shape
    nblk = page_table.shape[1] * page // MOBA_BLOCK
    assert 2 * page == MOBA_BLOCK and page_table.shape[1] * page == nblk * MOBA_BLOCK
    assert ns <= MOBA_BLOCK and nh % 8 == 0 and d == nh * HEAD_DIM
    rows = ns * nh
    pool_k = jnp.transpose(pool_k, (0, 2, 3, 1)).reshape(n_phys, d, page)
    pool_v = jnp.transpose(pool_v, (0, 2, 3, 1)).reshape(n_phys, d, page)
    new_spec = pl.BlockSpec((None, ns, d), lambda b, n, pt: (b, 0, 0))
    bps = MOBA_SAMPLE_BLOCKS_PER_STEP
    assert nblk % bps == 0
    pages_per_step = 2 * bps

    def page_spec(i):
        return pl.BlockSpec((None, d, page), lambda b, n, pt: (pt[b, pages_per_step * n + i], 0, 0))

    page_specs = [page_spec(i) for i in range(pages_per_step)]
    return pl.pallas_call(
        functools.partial(_moba_sample_kernel, nblk=nblk, ns=ns, nh=nh, blocks_per_step=bps),
        grid_spec=pltpu.PrefetchScalarGridSpec(
            num_scalar_prefetch=1,
            grid=(nb, nblk // bps),
            in_specs=[new_spec, new_spec, new_spec] + page_specs + page_specs,
            out_specs=new_spec,
            scratch_shapes=[
                pltpu.VMEM((rows, d), F32), pltpu.VMEM((2 * rows, d), BF16),
                pltpu.VMEM((nblk, rows, LANES), F32), pltpu.VMEM((nblk, rows, LANES), F32),
                pltpu.VMEM((nblk, rows, LANES), F32), pltpu.VMEM((nblk, rows, d), F32),
            ]),
        out_shape=jax.ShapeDtypeStruct((nb, ns, d), F32),
        compiler_params=_cparams("parallel", "arbitrary"),
        name="moba_sample",
    )(page_table, q, k_new, v_new, *([pool_k] * pages_per_step), *([pool_v] * pages_per_step))


def _matmul_residual_kernel(a_ref, w_ref, r_ref, o_ref):
    o_ref[...] = r_ref[...] + jnp.dot(a_ref[...].astype(BF16), w_ref[...].astype(BF16),
                                      preferred_element_type=F32)


def _matmul_residual(a, w, res):
    n, k = a.shape
    m = w.shape[1]
    tm = min(TM_TOKENS, n)
    assert n % tm == 0
    return pl.pallas_call(
        _matmul_residual_kernel,
        grid=(n // tm,),
        in_specs=[pl.BlockSpec((tm, k), lambda i: (i, 0)),
                  pl.BlockSpec((k, m), lambda i: (0, 0)),
                  pl.BlockSpec((tm, m), lambda i: (i, 0))],
        out_specs=pl.BlockSpec((tm, m), lambda i: (i, 0)),
        out_shape=jax.ShapeDtypeStruct((n, m), F32),
        compiler_params=_cparams("parallel"),
        name="matmul_residual",
    )(a, w, res)


def _dense_ffn_kernel(x_ref, g_ref, wg_ref, wu_ref, wd_ref, o_ref, h_ref):
    @pl.when(pl.program_id(1) == 0)
    def _():
        x = x_ref[...]
        h_ref[...] = _rmsnorm_rows(x, g_ref[...]).astype(BF16)
        o_ref[...] = x
    h = h_ref[...]
    gate = jnp.dot(h, wg_ref[...].astype(BF16), preferred_element_type=F32)
    up = jnp.dot(h, wu_ref[...].astype(BF16), preferred_element_type=F32)
    act = (gate * jax.nn.sigmoid(gate)) * up
    o_ref[...] += jnp.dot(act.astype(BF16), wd_ref[...].astype(BF16), preferred_element_type=F32)


def _dense_ffn(x, g, w_gu, w_down):
    n, d = x.shape
    f = w_down.shape[0]
    tm, tf = min(TM_FFN, n), TF_FFN
    assert n % tm == 0 and f % tf == 0
    nf = f // tf
    return pl.pallas_call(
        _dense_ffn_kernel,
        grid=(n // tm, nf),
        in_specs=[pl.BlockSpec((tm, d), lambda i, c: (i, 0)),
                  pl.BlockSpec((1, d), lambda i, c: (0, 0)),
                  pl.BlockSpec((d, tf), lambda i, c: (0, c)),
                  pl.BlockSpec((d, tf), lambda i, c: (0, nf + c)),
                  pl.BlockSpec((tf, d), lambda i, c: (c, 0))],
        out_specs=pl.BlockSpec((tm, d), lambda i, c: (i, 0)),
        out_shape=jax.ShapeDtypeStruct((n, d), F32),
        scratch_shapes=[pltpu.VMEM((tm, d), BF16)],
        compiler_params=_cparams("parallel", "arbitrary"),
        name="dense_ffn",
    )(x, g.reshape(1, d), w_gu, w_gu, w_down)


def _merge_groups(outs, lses):
    m = functools.reduce(jnp.maximum, lses)
    ws = [jnp.exp(l - m) for l in lses]
    num = functools.reduce(lambda a, b: a + b, [w * o for w, o in zip(ws, outs)])
    return num / functools.reduce(lambda a, b: a + b, ws)


def _dilated_prompt_kernel(q0_ref, q1_ref, q2_ref, k_ref, v_ref, o_ref,
                           qd_ref, kd_ref, vt_ref, od_ref, ld_ref, og_ref, lg_ref, *, s):
    span = DIL_SPAN
    ntiles = s // span
    head_of_row = lax.broadcasted_iota(I32, (LANES, span), 0) // HEAD_DIM
    key_i = lax.broadcasted_iota(I32, (span, span), 0)
    qry_i = lax.broadcasted_iota(I32, (span, span), 1)

    for g, (q_ref, (_, dil)) in enumerate(zip((q0_ref, q1_ref, q2_ref), DIL_GROUPS)):
        sub = s // dil
        tiles_per_class = sub // span
        for r in range(dil):
            src = pl.ds(r, sub, stride=dil) if dil > 1 else pl.ds(0, s)
            dst = slice(r * sub, (r + 1) * sub)
            qd_ref[dst, :] = q_ref[src, :] * ATTN_SCALE
            kd_ref[dst, :] = k_ref[src, :].astype(BF16)
            vr = v_ref[src, :]
            for tt in range(tiles_per_class):
                vt_ref[r * tiles_per_class + tt] = vr[tt * span:(tt + 1) * span, :].T.astype(BF16)

        def tile(t, carry, tiles_per_class=tiles_per_class):
            t_prev = jnp.maximum(t - 1, 0)
            cur = pl.ds(pl.multiple_of(t * span, span), span)
            prev = pl.ds(pl.multiple_of(t_prev * span, span), span)
            prev_off = jnp.where((t % tiles_per_class) != 0, 0, span)
            q_t = qd_ref[cur, :].T
            k2 = jnp.concatenate([kd_ref[prev, :], kd_ref[cur, :]], axis=0)
            vt2 = jnp.concatenate([vt_ref[t_prev], vt_ref[t]], axis=1)
            outs, lses = [], []
            for h in range(HEADS_PER_VREG):
                qh = jnp.where(head_of_row == h, q_t, 0.0).astype(BF16)
                sa = jnp.dot(k2, qh, preferred_element_type=F32)
                sp = jnp.where(key_i >= qry_i + prev_off, sa[:span], NEG_INF)
                sc = jnp.where(key_i <= qry_i, sa[span:], NEG_INF)
                sa = jnp.concatenate([sp, sc], axis=0)
                m = jnp.max(sa, axis=0, keepdims=True)
                p = jnp.exp(sa - m)
                l = jnp.sum(p, axis=0, keepdims=True)
                dims = slice(h * HEAD_DIM, (h + 1) * HEAD_DIM)
                outs.append(jnp.dot(vt2[dims, :], p.astype(BF16), preferred_element_type=F32) / l)
                lses.append(jnp.broadcast_to(m + jnp.log(l), (HEAD_DIM, span)))
            od_ref[cur, :] = jnp.concatenate(outs, axis=0).T
            ld_ref[cur, :] = jnp.concatenate(lses, axis=0).T
            return carry

        lax.fori_loop(0, ntiles, tile, 0, unroll=4)
        for r in range(dil):
            dst = pl.ds(r, sub, stride=dil) if dil > 1 else pl.ds(0, s)
            src = slice(r * sub, (r + 1) * sub)
            og_ref[g, dst, :] = od_ref[src, :]
            lg_ref[g, dst, :] = ld_ref[src, :]

    ng = len(DIL_GROUPS)
    o_ref[...] = _merge_groups([og_ref[g] for g in range(ng)], [lg_ref[g] for g in range(ng)])


def _dilated_prompt(q, k, v):
    b, s, hd = k.shape
    ng = len(DIL_GROUPS)
    assert q.shape == (b, s, ng * hd) and hd % LANES == 0
    assert all(s % (DIL_SPAN * dil) == 0 and win // dil == DIL_SPAN for win, dil in DIL_GROUPS)
    hp_blocks = hd // LANES

    def q_spec(g):
        return pl.BlockSpec((None, s, LANES), lambda bi, hp: (bi, 0, g * hp_blocks + hp))

    kv_spec = pl.BlockSpec((None, s, LANES), lambda bi, hp: (bi, 0, hp))
    return pl.pallas_call(
        functools.partial(_dilated_prompt_kernel, s=s),
        grid=(b, hp_blocks),
        in_specs=[q_spec(0), q_spec(1), q_spec(2), kv_spec, kv_spec],
        out_specs=kv_spec,
        out_shape=jax.ShapeDtypeStruct((b, s, hd), F32),
        scratch_shapes=[pltpu.VMEM((s, LANES), F32), pltpu.VMEM((s, LANES), BF16),
                        pltpu.VMEM((s // DIL_SPAN, LANES, DIL_SPAN), BF16),
                        pltpu.VMEM((s, LANES), F32), pltpu.VMEM((s, LANES), F32),
                        pltpu.VMEM((ng, s, LANES), F32), pltpu.VMEM((ng, s, LANES), F32)],
        compiler_params=_cparams("parallel", "parallel"),
        name="dilated_prompt",
    )(q, q, q, k, v)


def _dilated_sample_kernel(q_ref, kn_ref, vn_ref, kt_ref, vt_ref, o_ref, *, ns, nh, w_buf):
    d = nh * HEAD_DIM
    ng = len(DIL_GROUPS)
    rows = ng * ns * nh
    contract_last = (((1,), (1,)), ((), ()))

    q = q_ref[...]
    head_of_lane = lax.broadcasted_iota(I32, (nh, d), 1) // HEAD_DIM
    head_of_row = lax.broadcasted_iota(I32, (nh, d), 0)
    pieces = []
    for g in range(ng):
        for qi in range(ns):
            qrow = jnp.broadcast_to(q[qi:qi + 1, g * d:(g + 1) * d], (nh, d))
            pieces.append(jnp.where(head_of_lane == head_of_row, qrow, 0.0))
    qf = jnp.concatenate(pieces, axis=0)
    qb = (qf * ATTN_SCALE).astype(BF16)

    kn, vn = kn_ref[...], vn_ref[...]
    s_new = [jnp.sum(qf * kn[j:j + 1, :], axis=1, keepdims=True) * ATTN_SCALE for j in range(ns)]

    r1 = lax.broadcasted_iota(I32, (rows, 1), 0)
    g1, q1 = r1 // (ns * nh), (r1 // nh) % ns
    win_r = jnp.zeros((rows, 1), I32)
    low_bits_r = jnp.zeros((rows, 1), I32)
    for g, (win, dil) in enumerate(DIL_GROUPS):
        win_r = jnp.where(g1 == g, win, win_r)
        low_bits_r = jnp.where(g1 == g, dil - 1, low_bits_r)

    s = jnp.dot(qb, kt_ref[...].astype(BF16), preferred_element_type=F32)
    dist = (w_buf + q1) - lax.broadcasted_iota(I32, (rows, w_buf), 1)
    s = jnp.where(((dist & low_bits_r) == 0) & (dist <= win_r), s, NEG_INF)
    s_new = [jnp.where((q1 >= j) & (((q1 - j) & low_bits_r) == 0) & (q1 - j <= win_r), s_new[j], NEG_INF)
             for j in range(ns)]
    m = functools.reduce(jnp.maximum, [jnp.max(s, axis=1, keepdims=True)] + s_new)
    p = jnp.exp(s - m)
    l = jnp.sum(p, axis=1, keepdims=True)
    acc = lax.dot_general(p.astype(BF16), vt_ref[...].astype(BF16), contract_last, preferred_element_type=F32)
    for j in range(ns):
        w = jnp.exp(s_new[j] - m)
        l = l + w
        acc = acc + w * vn[j:j + 1, :]
    o_all = acc / l
    lse_all = jnp.broadcast_to(m + jnp.log(l), o_all.shape)

    own = (lax.broadcasted_iota(I32, o_all.shape, 1) // HEAD_DIM) == (lax.broadcasted_iota(I32, o_all.shape, 0) % nh)
    o_gq = jnp.sum(jnp.where(own, o_all, 0.0).reshape(ng * ns, nh, d), axis=1)
    lse_gq = jnp.sum(jnp.where(own, lse_all, 0.0).reshape(ng * ns, nh, d), axis=1)
    o_ref[...] = _merge_groups([o_gq[g * ns:(g + 1) * ns] for g in range(ng)],
                               [lse_gq[g * ns:(g + 1) * ns] for g in range(ng)])


def _dilated_sample(q, k_new, v_new, cache_k, cache_v):
    nb, ns, d = k_new.shape
    _, w_buf, nh, _ = cache_k.shape
    ng = len(DIL_GROUPS)
    assert all(dil & (dil - 1) == 0 for _, dil in DIL_GROUPS)
    assert d == nh * HEAD_DIM and nh % 8 == 0 and w_buf % LANES == 0 and q.shape == (nb, ns, ng * d)
    transposed = lambda c: jnp.transpose(c, (0, 2, 3, 1)).reshape(nb, d, w_buf)
    new_spec = pl.BlockSpec((None, ns, d), lambda b: (b, 0, 0))
    buf_spec = pl.BlockSpec((None, d, w_buf), lambda b: (b, 0, 0))
    return pl.pallas_call(
        functools.partial(_dilated_sample_kernel, ns=ns, nh=nh, w_buf=w_buf),
        grid=(nb,),
        in_specs=[pl.BlockSpec((None, ns, ng * d), lambda b: (b, 0, 0)), new_spec, new_spec, buf_spec, buf_spec],
        out_specs=new_spec,
        out_shape=jax.ShapeDtypeStruct((nb, ns, d), F32),
        compiler_params=_cparams("parallel"),
        name="dilated_sample",
    )(q, k_new, v_new, transposed(cache_k), transposed(cache_v))


def _router_kernel(x_ref, g_ref, wr_ref, h_ref, route_ref):
    h = _rmsnorm_rows(x_ref[...], g_ref[...])
    h_ref[...] = h
    logits = jnp.dot(h, wr_ref[...], precision=HIGHEST, preferred_element_type=F32)
    lane = lax.broadcasted_iota(I32, logits.shape, 1)
    lane_f = lane.astype(F32)
    l1 = jnp.where(lane < N_EXPERTS, logits, NEG_INF)
    m1 = jnp.max(l1, axis=1, keepdims=True)
    i1 = jnp.min(jnp.where(l1 == m1, lane_f, float(LANES)), axis=1, keepdims=True)
    l2 = jnp.where(lane_f == i1, NEG_INF, l1)
    m2 = jnp.max(l2, axis=1, keepdims=True)
    i2 = jnp.min(jnp.where(l2 == m2, lane_f, float(LANES)), axis=1, keepdims=True)
    e = jnp.exp(m2 - m1)
    den = 1.0 + e
    packed = jnp.where(lane == 0, i1, jnp.where(lane == 1, i2,
             jnp.where(lane == 2, 1.0 / den, jnp.where(lane == 3, e / den, 0.0))))
    route_ref[...] = packed[:, :route_ref.shape[1]]


ROUTE_COLS = 8


def _router(x, g, w_router_padded):
    n, d = x.shape
    tm = min(TM_ROUTE, n)
    assert n % tm == 0
    return pl.pallas_call(
        _router_kernel,
        grid=(n // tm,),
        in_specs=[pl.BlockSpec((tm, d), lambda i: (i, 0)),
                  pl.BlockSpec((1, d), lambda i: (0, 0)),
                  pl.BlockSpec((d, LANES), lambda i: (0, 0))],
        out_specs=[pl.BlockSpec((tm, d), lambda i: (i, 0)),
                   pl.BlockSpec((tm, ROUTE_COLS), lambda i: (i, 0))],
        out_shape=[jax.ShapeDtypeStruct((n, d), F32), jax.ShapeDtypeStruct((n, ROUTE_COLS), F32)],
        compiler_params=_cparams("parallel"),
        name="moe_router",
    )(x, g.reshape(1, d), w_router_padded)


def _dispatch_kernel(dest_ref, h_ref, slots_in, slots_out, sem, *, tm):
    del slots_in

    def row_copy(r, k):
        return pltpu.make_async_copy(h_ref.at[pl.ds(r, 1)],
                                     slots_out.at[pl.ds(dest_ref[0, TOP_K * r + k], 1)], sem)

    def issue(r, c):
        for k in range(TOP_K):
            row_copy(r, k).start()
        return c

    def drain(r, c):
        for k in range(TOP_K):
            row_copy(r, k).wait()
        return c

    lax.fori_loop(0, tm, issue, 0, unroll=8)
    lax.fori_loop(0, tm, drain, 0, unroll=8)


def _dispatch(h, dest, slots):
    n, d = h.shape
    tm = min(TM_ROUTE, n)
    assert n % tm == 0
    return pl.pallas_call(
        functools.partial(_dispatch_kernel, tm=tm),
        grid=(n // tm,),
        in_specs=[pl.BlockSpec((None, 1, TOP_K * tm), lambda i: (i, 0, 0), memory_space=pltpu.SMEM),
                  pl.BlockSpec((tm, d), lambda i: (i, 0)),
                  pl.BlockSpec(memory_space=pl.ANY)],
        out_specs=pl.BlockSpec(memory_space=pl.ANY),
        out_shape=jax.ShapeDtypeStruct(slots.shape, slots.dtype),
        scratch_shapes=[pltpu.SemaphoreType.DMA(())],
        input_output_aliases={2: 0},
        compiler_params=_cparams("arbitrary"),
        name="moe_dispatch",
    )(dest.reshape(n // tm, 1, TOP_K * tm), h, slots)


def _moe_gu_kernel(exp_ref, valid_ref, first_ref, x_ref, wg_ref, wu_ref, o_ref, wgb_ref, wub_ref):
    del exp_ref
    t = pl.program_id(1)

    @pl.when(first_ref[t] == 1)
    def _():
        wgb_ref[...] = wg_ref[...].astype(BF16)
        wub_ref[...] = wu_ref[...].astype(BF16)

    @pl.when(valid_ref[t] == 1)
    def _():
        x = x_ref[...].astype(BF16)
        gate = jnp.dot(x, wgb_ref[...], preferred_element_type=F32)
        up = jnp.dot(x, wub_ref[...], preferred_element_type=F32)
        o_ref[...] = ((gate * jax.nn.sigmoid(gate)) * up).astype(BF16)

    @pl.when(valid_ref[t] == 0)
    def _():
        o_ref[...] = jnp.zeros(o_ref.shape, o_ref.dtype)


def _moe_gu(slots, w_gu, tile_exp, tile_valid, tile_first):
    r, d = slots.shape
    f = w_gu.shape[2] // 2
    tm, tn = TM_MOE, TN_MOE_GU
    assert r % tm == 0 and f % tn == 0
    nj = f // tn
    return pl.pallas_call(
        _moe_gu_kernel,
        grid_spec=pltpu.PrefetchScalarGridSpec(
            num_scalar_prefetch=3,
            grid=(nj, r // tm),
            in_specs=[pl.BlockSpec((tm, d), lambda j, t, ex, va, fi: (t, 0)),
                      pl.BlockSpec((None, d, tn), lambda j, t, ex, va, fi: (ex[t], 0, j)),
                      pl.BlockSpec((None, d, tn), lambda j, t, ex, va, fi: (ex[t], 0, nj + j))],
            out_specs=pl.BlockSpec((tm, tn), lambda j, t, ex, va, fi: (t, j)),
            scratch_shapes=[pltpu.VMEM((d, tn), BF16), pltpu.VMEM((d, tn), BF16)]),
        out_shape=jax.ShapeDtypeStruct((r, f), BF16),
        compiler_params=_cparams("arbitrary", "arbitrary"),
        name="moe_gate_up",
    )(tile_exp, tile_valid, tile_first, slots, w_gu, w_gu)


def _moe_down_kernel(exp_ref, valid_ref, first_ref, a_ref, wd_ref, o_ref, wdb_ref):
    del exp_ref
    t = pl.program_id(1)

    @pl.when(first_ref[t] == 1)
    def _():
        wdb_ref[...] = wd_ref[...].astype(BF16)

    @pl.when(valid_ref[t] == 1)
    def _():
        o_ref[...] = jnp.dot(a_ref[...], wdb_ref[...], preferred_element_type=F32)

    @pl.when(valid_ref[t] == 0)
    def _():
        o_ref[...] = jnp.zeros(o_ref.shape, o_ref.dtype)


def _moe_down(act, w_down, tile_exp, tile_valid, tile_first):
    r, f = act.shape
    d = w_down.shape[2]
    tm, tn = TM_MOE, TN_MOE_DOWN
    assert r % tm == 0 and d % tn == 0
    return pl.pallas_call(
        _moe_down_kernel,
        grid_spec=pltpu.PrefetchScalarGridSpec(
            num_scalar_prefetch=3,
            grid=(d // tn, r // tm),
            in_specs=[pl.BlockSpec((tm, f), lambda j, t, ex, va, fi: (t, 0)),
                      pl.BlockSpec((None, f, tn), lambda j, t, ex, va, fi: (ex[t], 0, j))],
            out_specs=pl.BlockSpec((tm, tn), lambda j, t, ex, va, fi: (t, j)),
            scratch_shapes=[pltpu.VMEM((f, tn), BF16)]),
        out_shape=jax.ShapeDtypeStruct((r, d), F32),
        compiler_params=_cparams("arbitrary", "arbitrary"),
        name="moe_down",
    )(tile_exp, tile_valid, tile_first, act, w_down)


def _combine_kernel(dest_ref, x_ref, route_ref, rows_hbm, y_ref, buf_ref, sem, *, tm):
    def row_copy(r, k):
        return pltpu.make_async_copy(rows_hbm.at[pl.ds(dest_ref[0, TOP_K * r + k], 1)],
                                     buf_ref.at[k, pl.ds(r, 1)], sem)

    def issue(r, c):
        for k in range(TOP_K):
            row_copy(r, k).start()
        return c

    def drain(r, c):
        for k in range(TOP_K):
            row_copy(r, k).wait()
        return c

    lax.fori_loop(0, tm, issue, 0, unroll=8)
    lax.fori_loop(0, tm, drain, 0, unroll=8)
    route = route_ref[...]
    y_ref[...] = x_ref[...] + (route[:, 2:3] * buf_ref[0] + route[:, 3:4] * buf_ref[1])


def _combine(x, route, dest, rows):
    n, d = x.shape
    tm = min(TM_ROUTE, n)
    assert n % tm == 0
    return pl.pallas_call(
        functools.partial(_combine_kernel, tm=tm),
        grid=(n // tm,),
        in_specs=[pl.BlockSpec((None, 1, TOP_K * tm), lambda i: (i, 0, 0), memory_space=pltpu.SMEM),
                  pl.BlockSpec((tm, d), lambda i: (i, 0)),
                  pl.BlockSpec((tm, ROUTE_COLS), lambda i: (i, 0)),
                  pl.BlockSpec(memory_space=pl.ANY)],
        out_specs=pl.BlockSpec((tm, d), lambda i: (i, 0)),
        out_shape=jax.ShapeDtypeStruct((n, d), F32),
        scratch_shapes=[pltpu.VMEM((TOP_K, tm, d), F32), pltpu.SemaphoreType.DMA(())],
        compiler_params=_cparams("arbitrary"),
        name="moe_combine",
    )(dest.reshape(n // tm, 1, TOP_K * tm), x, route, rows)


def _moe_ffn(xs, g, w_router, w_gu, w_down):
    d = xs[0].shape[1]
    wr = jnp.zeros((d, LANES), F32).at[:, :N_EXPERTS].set(w_router)
    routed = [_router(x, g, wr) for x in xs]
    e_flat = jnp.concatenate([route[:, :TOP_K] for _, route in routed], axis=0).astype(I32).reshape(-1)
    a = e_flat.shape[0]
    onehot = (e_flat[:, None] == jnp.arange(N_EXPERTS, dtype=I32)[None, :]).astype(I32)
    csum = jnp.cumsum(onehot, axis=0)
    rank = jnp.sum(csum * onehot, axis=1) - 1
    counts = csum[-1]
    padded = (counts + TM_MOE - 1) // TM_MOE * TM_MOE
    pends = jnp.cumsum(padded)
    dest = (jnp.sum((pends - padded)[None, :] * onehot, axis=1) + rank).astype(I32)
    n_tiles = -(-a // TM_MOE) + N_EXPERTS
    tile_start = jnp.arange(n_tiles, dtype=I32) * TM_MOE
    tile_exp = jnp.minimum(jnp.sum(tile_start[:, None] >= pends[None, :], axis=1), N_EXPERTS - 1).astype(I32)
    tile_valid = (tile_start < pends[-1]).astype(I32)
    tile_first = jnp.concatenate([jnp.ones((1,), I32), (tile_exp[1:] != tile_exp[:-1]).astype(I32)])

    slots = jnp.zeros((n_tiles * TM_MOE, d), F32)
    dests, off = [], 0
    for h, _ in routed:
        dests.append(dest[off:off + TOP_K * h.shape[0]])
        slots = _dispatch(h, dests[-1], slots)
        off += TOP_K * h.shape[0]
    act = _moe_gu(slots, w_gu, tile_exp, tile_valid, tile_first)
    rows = _moe_down(act, w_down, tile_exp, tile_valid, tile_first)
    return [_combine(x, route, dst, rows) for x, (_, route), dst in zip(xs, routed, dests)]


def kernel(x_prompt, x_sample, cache_a_k, cache_a_v, cache_b_k, cache_b_v, page_table, norm_attn_a, w_qkv_a,
           qnorm_a, knorm_a, w_o_a, norm_kv_b, w_kv_b, knorm_b, norm_attn_b, w_q_b, qnorm_b, w_o_b, norm_ffn,
           w_gu_dense, w_down_dense, w_router, w_gu_moe, w_down_moe):
    b, s, d = x_prompt.shape
    nb, ns, _ = x_sample.shape
    depth, n_a = norm_ffn.shape[0], w_qkv_a.shape[0]
    n_phys, page = cache_a_k.shape[1], cache_a_k.shape[2]
    past = page_table.shape[1] * page
    heads_a = d // HEAD_DIM
    d_b = w_kv_b.shape[1] // 2
    heads_b = d_b // HEAD_DIM
    n_groups = len(DIL_GROUPS)
    w_buf = cache_b_k.shape[1]
    w_max = max(w for w, _ in DIL_GROUPS)

    xp = x_prompt.reshape(b * s, d)
    xs = x_sample.reshape(nb * ns, d)
    tabs_p = _rope_tables(jnp.tile(jnp.arange(s), b))
    tabs_s = _rope_tables(jnp.tile(past + jnp.arange(ns), nb))
    ones_d = jnp.ones((d,), F32)
    ak_p, av_p, ak_s, av_s = [], [], [], []
    for l in range(depth):
        if l < n_a:
            gains = jnp.concatenate([jnp.tile(qnorm_a[l], heads_a), jnp.tile(knorm_a[l], heads_a), ones_d])
            qp, kp, kp_t, vp, vp_t = _norm_proj(xp, norm_attn_a[l], w_qkv_a[l], gains, tabs_p,
                                                ((d, True, False), (d, True, True), (d, False, True)), seq=s)
            qs, ks, vs = _norm_proj(xs, norm_attn_a[l], w_qkv_a[l], gains, tabs_s,
                                    ((d, True, False), (d, True, False), (d, False, False)))
            op = _moba_prompt(qp.reshape(b, s, d), kp.reshape(b, s, d), vp.reshape(b, s, d))
            os_ = _moba_sample(qs.reshape(nb, ns, d), ks.reshape(nb, ns, d), vs.reshape(nb, ns, d),
                               cache_a_k[l], cache_a_v[l], page_table)
            xp = _matmul_residual(op.reshape(b * s, d), w_o_a[l], xp)
            xs = _matmul_residual(os_.reshape(nb * ns, d), w_o_a[l], xs)
            ak_p.append(kp_t)
            av_p.append(vp_t)
            ak_s.append(ks)
            av_s.append(vs)
        else:
            j = l - n_a
            gains = jnp.tile(qnorm_b[j], n_groups * heads_b)
            parts = ((n_groups * d_b, True, False),)
            (qp,) = _norm_proj(xp, norm_attn_b[j], w_q_b[j], gains, tabs_p, parts)
            (qs,) = _norm_proj(xs, norm_attn_b[j], w_q_b[j], gains, tabs_s, parts)
            op = _dilated_prompt(qp.reshape(b, s, n_groups * d_b), kb_p.reshape(b, s, d_b), vb_p.reshape(b, s, d_b))
            os_ = _dilated_sample(qs.reshape(nb, ns, n_groups * d_b), kb_s.reshape(nb, ns, d_b),
                                  vb_s.reshape(nb, ns, d_b), cache_b_k, cache_b_v)
            xp = _matmul_residual(op.reshape(b * s, d_b), w_o_b[j], xp)
            xs = _matmul_residual(os_.reshape(nb * ns, d_b), w_o_b[j], xs)
        if l % 2 == 0:
            xp = _dense_ffn(xp, norm_ffn[l], w_gu_dense[l // 2], w_down_dense[l // 2])
            xs = _dense_ffn(xs, norm_ffn[l], w_gu_dense[l // 2], w_down_dense[l // 2])
        else:
            xp, xs = _moe_ffn([xp, xs], norm_ffn[l], w_router[l // 2], w_gu_moe[l // 2], w_down_moe[l // 2])
        if l == n_a - 1:
            gains = jnp.concatenate([jnp.tile(knorm_b, heads_b), jnp.ones((d_b,), F32)])
            kb_p, kb_pt, vb_p, vb_pt = _norm_proj(xp, norm_kv_b, w_kv_b, gains, tabs_p,
                                                  ((d_b, True, True), (d_b, False, True)), seq=s)
            kb_s, vb_s = _norm_proj(xs, norm_kv_b, w_kv_b, gains, tabs_s, ((d_b, True, False), (d_b, False, False)))

    keep = min(w_max, s)
    heads = lambda t, n, h: t.reshape(n_a, *n, h, HEAD_DIM)
    seq_major = lambda t, h: jnp.moveaxis(t.reshape(*t.shape[:-2], h, HEAD_DIM, t.shape[-1]), -1, -3)
    return (xp.reshape(b, s, d), xs.reshape(nb, ns, d),
            seq_major(jnp.stack(ak_p), heads_a), seq_major(jnp.stack(av_p), heads_a),
            seq_major(kb_pt, heads_b)[:, s - keep:], seq_major(vb_pt, heads_b)[:, s - keep:],
            heads(jnp.stack(ak_s), (nb, ns), heads_a), heads(jnp.stack(av_s), (nb, ns), heads_a),
            kb_s.reshape(nb, ns, heads_b, HEAD_DIM), vb_s.reshape(nb, ns, heads_b, HEAD_DIM))
```

```python
import functools

import jax
import jax.numpy as jnp
from jax import lax
from jax.experimental import pallas as pl
from jax.experimental.pallas import tpu as pltpu

F32 = jnp.float32
BF16 = jnp.bfloat16
I32 = jnp.int32
HIGHEST = lax.Precision.HIGHEST
NEG_INF = float("-inf")

HEAD_DIM = 64
MOBA_BLOCK = 256
MOBA_TOPK = 3
DIL_GROUPS = ((128, 1), (512, 4), (2048, 16))
DIL_SPAN = 128
ROT_DIM = HEAD_DIM // 4
ROPE_THETA = 500000.0
ATTN_SCALE = HEAD_DIM ** -0.5
RMS_EPS = 1e-6
N_EXPERTS = 8
TOP_K = 2

LANES = 128
HEADS_PER_VREG = LANES // HEAD_DIM
VMEM_LIMIT_BYTES = 56 * 1024 * 1024

TM_TOKENS = 512
TM_PROJ = 1024
TN_PROJ = 512
MOBA_SAMPLE_BLOCKS_PER_STEP = 8
TM_FFN = 1024
TF_FFN = 256
TM_MOE = 512
TN_MOE_GU = 1792
TN_MOE_DOWN = 1024
TM_ROUTE = 512


def _cparams(*sem):
    return pltpu.CompilerParams(dimension_semantics=sem, vmem_limit_bytes=VMEM_LIMIT_BYTES)


def _rmsnorm_rows(x, g):
    ms = jnp.mean(x * x, axis=-1, keepdims=True)
    return x * lax.rsqrt(ms + RMS_EPS) * g


MXU_COLS = 256


def _norm_proj_kernel(x_ref, g_ref, w_ref, hg_ref, c_ref, s1_ref, s2_ref, bd_ref, *rest, parts, tn):
    h_ref = rest[-1]
    out_refs = iter(rest[:-1])
    j = pl.program_id(1)

    @pl.when(j == 0)
    def _():
        h_ref[...] = _rmsnorm_rows(x_ref[...], g_ref[...]).astype(BF16)

    def head_norm_rope(y, cols):
        ss = jnp.dot((y * y).astype(BF16), bd_ref[...], preferred_element_type=F32)
        yn = y * lax.rsqrt(ss * (1.0 / HEAD_DIM) + RMS_EPS) * hg_ref[:, cols]
        c, s1, s2 = c_ref[...], s1_ref[...], s2_ref[...]
        halves = []
        for hb in range(MXU_COLS // LANES):
            yh = yn[:, hb * LANES:(hb + 1) * LANES]
            up = pltpu.roll(yh, LANES - ROT_DIM // 2, 1)
            dn = pltpu.roll(yh, ROT_DIM // 2, 1)
            halves.append(yh * c + up * s1 + dn * s2)
        return jnp.concatenate(halves, axis=1)

    for start, ntiles, normed, transposed in parts:
        o_ref = next(out_refs)
        ot_ref = next(out_refs) if transposed else None

        @pl.when((j >= start) & (j < start + ntiles))
        def _(o_ref=o_ref, ot_ref=ot_ref, normed=normed):
            h = h_ref[...]
            for pb in range(tn // MXU_COLS):
                cols = slice(pb * MXU_COLS, (pb + 1) * MXU_COLS)
                y = jnp.dot(h, w_ref[:, cols].astype(BF16), preferred_element_type=F32)
                if normed:
                    y = head_norm_rope(y, cols)
                o_ref[:, cols] = y
                if ot_ref is not None:
                    ot_ref[cols, :] = y.T


def _norm_proj(x, g, w, head_gain, rope_tabs, parts, seq=None):
    n, d = x.shape
    m = w.shape[1]
    tm, tn = min(TM_PROJ, n), TN_PROJ
    assert n % tm == 0 and m % tn == 0 and tn % MXU_COLS == 0 and all(p[0] % tn == 0 for p in parts)
    tile_parts, start = [], 0
    for width, normed, transposed in parts:
        tile_parts.append((start, width // tn, normed, transposed))
        start += width // tn
    c, s1, s2 = rope_tabs
    bd = (jnp.arange(MXU_COLS)[:, None] // HEAD_DIM == jnp.arange(MXU_COLS)[None, :] // HEAD_DIM).astype(BF16)

    out_specs, out_shape = [], []
    for (st, nt, _, transposed), (width, _, _) in zip(tile_parts, parts):
        col = lambda j, st=st, nt=nt: jnp.clip(j - st, 0, nt - 1)
        out_specs.append(pl.BlockSpec((tm, tn), lambda i, j, col=col: (i, col(j))))
        out_shape.append(jax.ShapeDtypeStruct((n, width), F32))
        if transposed:
            assert seq is not None and seq % tm == 0 and n % seq == 0
            per_seq = seq // tm
            out_specs.append(pl.BlockSpec((None, tn, tm),
                                          lambda i, j, col=col: (i // per_seq, col(j), i % per_seq)))
            out_shape.append(jax.ShapeDtypeStruct((n // seq, width, seq), F32))

    tab_spec = pl.BlockSpec((tm, LANES), lambda i, j: (i, 0))
    return pl.pallas_call(
        functools.partial(_norm_proj_kernel, parts=tuple(tile_parts), tn=tn),
        grid=(n // tm, m // tn),
        in_specs=[
            pl.BlockSpec((tm, d), lambda i, j: (i, 0)),
            pl.BlockSpec((1, d), lambda i, j: (0, 0)),
            pl.BlockSpec((d, tn), lambda i, j: (0, j)),
            pl.BlockSpec((1, tn), lambda i, j: (0, j)),
            tab_spec, tab_spec, tab_spec,
            pl.BlockSpec((MXU_COLS, MXU_COLS), lambda i, j: (0, 0)),
        ],
        out_specs=out_specs,
        out_shape=out_shape,
        scratch_shapes=[pltpu.VMEM((tm, d), BF16)],
        compiler_params=_cparams("parallel", "arbitrary"),
        name="norm_proj",
    )(x, g.reshape(1, d), w, head_gain.reshape(1, m), c, s1, s2, bd)


def _rope_tables(pos):
    half = ROT_DIM // 2
    inv = ROPE_THETA ** (-2.0 * jnp.arange(half, dtype=F32) / ROT_DIM)
    ang = pos.astype(F32)[:, None] * inv[None, :]
    cos, sin = jnp.cos(ang), jnp.sin(ang)
    n = pos.shape[0]
    zeros = jnp.zeros((n, HEAD_DIM - ROT_DIM), F32)
    zh = jnp.zeros((n, half), F32)
    c = jnp.concatenate([cos, cos, jnp.ones((n, HEAD_DIM - ROT_DIM), F32)], axis=1)
    s1 = jnp.concatenate([-sin, zh, zeros], axis=1)
    s2 = jnp.concatenate([zh, sin, zeros], axis=1)
    tile = lambda t: jnp.tile(t, (1, HEADS_PER_VREG))
    return tile(c), tile(s1), tile(s2)


def _topk_block_mask(gate_t, n_valid):
    nblk = gate_t.shape[0]
    blk_idx = lax.broadcasted_iota(I32, gate_t.shape, 0)
    rank = jnp.zeros(gate_t.shape, I32)
    for m in range(nblk):
        gm = gate_t[m:m + 1, :]
        beats = (gm > gate_t) | ((gm == gate_t) & (m < blk_idx))
        rank = rank + jnp.where(beats & (m < n_valid), 1, 0)
    return jnp.where((blk_idx < n_valid) & (rank < MOBA_TOPK), 1.0, 0.0)


def _moba_prompt_kernel(q_ref, k_ref, v_ref, o_ref, kb_ref, vt_ref, vt1_ref, km_ref, *, nblk, nh_step):
    qi = pl.program_id(2)
    blk = MOBA_BLOCK

    @pl.when(qi == 0)
    def _():
        for n in range(nblk):
            rows = slice(n * blk, (n + 1) * blk)
            half = slice((n % 2) * blk, (n % 2 + 1) * blk)
            kn = k_ref[rows, :]
            kb_ref[n // 2, half, :] = kn.astype(BF16)
            vn_t = v_ref[rows, :].T.astype(BF16)
            vt_ref[n // 2, :, half] = vn_t
            vt1_ref[n] = vn_t
            km_ref[n:n + 1, :] = jnp.mean(kn, axis=0, keepdims=True)

    q_t = q_ref[...].T
    head_of_row = lax.broadcasted_iota(I32, (nh_step * HEAD_DIM, blk), 0) // HEAD_DIM
    causal = lax.broadcasted_iota(I32, (blk, blk), 0) <= lax.broadcasted_iota(I32, (blk, blk), 1)
    blk_idx = lax.broadcasted_iota(I32, (nblk, blk), 0)
    km = km_ref[...]
    k_own = kb_ref[qi // 2, pl.ds(pl.multiple_of((qi % 2) * blk, blk), blk), :]
    vt_own = vt1_ref[qi]

    qs, sel_t, init = [], [], []
    for h in range(nh_step):
        qh = jnp.where(head_of_row == h, q_t, 0.0)
        gate_t = jnp.dot(km, qh, precision=HIGHEST, preferred_element_type=F32)
        sel_t.append(_topk_block_mask(gate_t, qi))
        qs.append((qh * ATTN_SCALE).astype(BF16))
        s = jnp.where(causal, jnp.dot(k_own, qs[h], preferred_element_type=F32), NEG_INF)
        m = jnp.max(s, axis=0, keepdims=True)
        p = jnp.exp(s - m)
        dims = slice(h * HEAD_DIM, (h + 1) * HEAD_DIM)
        init.append((m, jnp.sum(p, axis=0, keepdims=True),
                     jnp.dot(vt_own[dims, :], p.astype(BF16), preferred_element_type=F32)))

    def block_pair(pi, carry):
        k2 = kb_ref[pi]
        vt2 = vt_ref[pi]
        new = []
        for h in range(nh_step):
            m, l, a = carry[h]
            s = jnp.dot(k2, qs[h], preferred_element_type=F32)
            vis = [jnp.max(jnp.where(blk_idx == 2 * pi + c, sel_t[h], 0.0), axis=0, keepdims=True) > 0.0
                   for c in range(2)]
            col_max = [jnp.max(s[c * blk:(c + 1) * blk], axis=0, keepdims=True) for c in range(2)]
            m_new = functools.reduce(jnp.maximum, [m] + [jnp.where(v, cm, NEG_INF) for v, cm in zip(vis, col_max)])
            alpha = jnp.exp(m - m_new)
            p = jnp.concatenate([jnp.exp(s[c * blk:(c + 1) * blk] - jnp.where(vis[c], m_new, jnp.inf))
                                 for c in range(2)], axis=0)
            l = alpha * l + jnp.sum(p, axis=0, keepdims=True)
            dims = slice(h * HEAD_DIM, (h + 1) * HEAD_DIM)
            a = alpha * a + jnp.dot(vt2[dims, :], p.astype(BF16), preferred_element_type=F32)
            new.append((m_new, l, a))
        return tuple(new)

    final = lax.fori_loop(0, (qi + 1) // 2, block_pair, tuple(init))
    o_ref[...] = jnp.concatenate([a / l for _, l, a in final], axis=0).T


MOBA_HEADS_PER_STEP = 4


def _moba_prompt(q, k, v):
    b, s, hd = q.shape
    nblk = s // MOBA_BLOCK
    w = MOBA_HEADS_PER_STEP * HEAD_DIM
    assert s % (2 * MOBA_BLOCK) == 0 and hd % w == 0 and w % LANES == 0
    q_spec = pl.BlockSpec((None, MOBA_BLOCK, w), lambda bi, hp, qi: (bi, qi, hp))
    kv_spec = pl.BlockSpec((None, s, w), lambda bi, hp, qi: (bi, 0, hp))
    return pl.pallas_call(
        functools.partial(_moba_prompt_kernel, nblk=nblk, nh_step=MOBA_HEADS_PER_STEP),
        grid=(b, hd // w, nblk),
        in_specs=[q_spec, kv_spec, kv_spec],
        out_specs=q_spec,
        out_shape=jax.ShapeDtypeStruct((b, s, hd), F32),
        scratch_shapes=[pltpu.VMEM((nblk // 2, 2 * MOBA_BLOCK, w), BF16),
                        pltpu.VMEM((nblk // 2, w, 2 * MOBA_BLOCK), BF16),
                        pltpu.VMEM((nblk, w, MOBA_BLOCK), BF16),
                        pltpu.VMEM((nblk, w), F32)],
        compiler_params=_cparams("parallel", "parallel", "arbitrary"),
        name="moba_prompt",
    )(q, k, v)


def _moba_sample_kernel(pt_ref, q_ref, kn_ref, vn_ref, *rest, nblk, ns, nh, blocks_per_step):
    del pt_ref
    n_pages = 2 * blocks_per_step
    k_pages, v_pages = rest[:n_pages], rest[n_pages:2 * n_pages]
    o_ref, qf_ref, q2_ref, g_ref, m_ref, l_ref, acc_ref = rest[2 * n_pages:]
    step = pl.program_id(1)
    n = step * blocks_per_step
    rows = ns * nh
    d = nh * HEAD_DIM
    page = MOBA_BLOCK // 2
    contract_last = (((1,), (1,)), ((), ()))

    @pl.when(step == 0)
    def _():
        q = q_ref[...]
        head_of_lane = lax.broadcasted_iota(I32, (nh, d), 1) // HEAD_DIM
        head_of_row = lax.broadcasted_iota(I32, (nh, d), 0)
        for qi in range(ns):
            piece = jnp.where(head_of_lane == head_of_row, jnp.broadcast_to(q[qi:qi + 1, :], (nh, d)), 0.0)
            hi = piece.astype(BF16)
            qf_ref[qi * nh:(qi + 1) * nh, :] = piece
            q2_ref[qi * nh:(qi + 1) * nh, :] = (piece * ATTN_SCALE).astype(BF16)
            q2_ref[rows + qi * nh:rows + (qi + 1) * nh, :] = (piece - hi.astype(F32)).astype(BF16)

    q2 = q2_ref[...]

    def page_scores(k_page_ref):
        k = k_page_ref[...]
        k_hi = k.astype(BF16)
        k_lo = (k - k_hi.astype(F32)).astype(BF16)
        s2 = jnp.dot(q2, k_hi, preferred_element_type=F32)
        s_lo = jnp.dot(q2[:rows], k_lo, preferred_element_type=F32)
        return s2[:rows], (s2[:rows] + s_lo) * (1.0 / ATTN_SCALE) + s2[rows:]

    for c in range(blocks_per_step):
        sa, ga = page_scores(k_pages[2 * c])
        sb, gb = page_scores(k_pages[2 * c + 1])
        s = jnp.concatenate([sa, sb], axis=1)
        gate = (jnp.sum(ga, axis=1, keepdims=True) + jnp.sum(gb, axis=1, keepdims=True)) * (1.0 / MOBA_BLOCK)
        m = jnp.max(s, axis=1, keepdims=True)
        p = jnp.exp(s - m)
        l = jnp.sum(p, axis=1, keepdims=True)
        pb = p.astype(BF16)
        g_ref[n + c] = jnp.broadcast_to(gate, (rows, LANES))
        m_ref[n + c] = jnp.broadcast_to(m, (rows, LANES))
        l_ref[n + c] = jnp.broadcast_to(l, (rows, LANES))
        acc_ref[n + c] = (
            lax.dot_general(pb[:, :page], v_pages[2 * c][...].astype(BF16), contract_last, preferred_element_type=F32)
            + lax.dot_general(pb[:, page:], v_pages[2 * c + 1][...].astype(BF16), contract_last,
                              preferred_element_type=F32))

    @pl.when(step == nblk // blocks_per_step - 1)
    def _():
        sel = []
        for a in range(nblk):
            ga = g_ref[a]
            rank = jnp.zeros((rows, LANES), I32)
            for b in range(nblk):
                gb = g_ref[b]
                beats = (gb > ga) if b > a else (gb >= ga)
                if b != a:
                    rank = rank + jnp.where(beats, 1, 0)
            sel.append(rank < MOBA_TOPK)
        qf = qf_ref[...]
        kn, vn = kn_ref[...], vn_ref[...]
        q_of_row = lax.broadcasted_iota(I32, (rows, 1), 0) // nh
        s_own = [jnp.sum(qf * kn[j:j + 1, :], axis=1, keepdims=True) * ATTN_SCALE for j in range(ns)]
        s_own = [jnp.where(j <= q_of_row, s_own[j], NEG_INF) for j in range(ns)]
        m_blk = [jnp.where(sel[a], m_ref[a], NEG_INF)[:, 0:1] for a in range(nblk)]
        m_all = functools.reduce(jnp.maximum, s_own + m_blk)
        den = jnp.zeros((rows, 1), F32)
        num = jnp.zeros((rows, d), F32)
        for j in range(ns):
            w = jnp.exp(s_own[j] - m_all)
            den = den + w
            num = num + w * vn[j:j + 1, :]
        for a in range(nblk):
            w = jnp.exp(m_blk[a] - m_all)
            den = den + w * l_ref[a][:, 0:1]
            num = num + w * acc_ref[a]
        res = num / den
        head_of_lane = lax.broadcasted_iota(I32, (rows, d), 1) // HEAD_DIM
        head_of_row = lax.broadcasted_iota(I32, (rows, d), 0) % nh
        res = jnp.where(head_of_lane == head_of_row, res, 0.0)
        o_ref[...] = jnp.sum(res.reshape(ns, nh, d), axis=1)


def _moba_sample(q, k_new, v_new, pool_k, pool_v, page_table):
    nb, ns, d = q.shape
    n_phys, page, nh, _ = pool_k.shape
    nblk = page_table.shape[1] * page // MOBA_BLOCK
    assert 2 * page == MOBA_BLOCK and page_table.shape[1] * page == nblk * MOBA_BLOCK
    assert ns <= MOBA_BLOCK and nh % 8 == 0 and d == nh * HEAD_DIM
    rows = ns * nh
    pool_k = jnp.transpose(pool_k, (0, 2, 3, 1)).reshape(n_phys, d, page)
    pool_v = jnp.transpose(pool_v, (0, 2, 3, 1)).reshape(n_phys, d, page)
    new_spec = pl.BlockSpec((None, ns, d), lambda b, n, pt: (b, 0, 0))
    bps = MOBA_SAMPLE_BLOCKS_PER_STEP
    assert nblk % bps == 0
    pages_per_step = 2 * bps

    def page_spec(i):
        return pl.BlockSpec((None, d, page), lambda b, n, pt: (pt[b, pages_per_step * n + i], 0, 0))

    page_specs = [page_spec(i) for i in range(pages_per_step)]
    return pl.pallas_call(
        functools.partial(_moba_sample_kernel, nblk=nblk, ns=ns, nh=nh, blocks_per_step=bps),
        grid_spec=pltpu.PrefetchScalarGridSpec(
            num_scalar_prefetch=1,
            grid=(nb, nblk // bps),
            in_specs=[new_spec, new_spec, new_spec] + page_specs + page_specs,
            out_specs=new_spec,
            scratch_shapes=[
                pltpu.VMEM((rows, d), F32), pltpu.VMEM((2 * rows, d), BF16),
                pltpu.VMEM((nblk, rows, LANES), F32), pltpu.VMEM((nblk, rows, LANES), F32),
                pltpu.VMEM((nblk, rows, LANES), F32), pltpu.VMEM((nblk, rows, d), F32),
            ]),
        out_shape=jax.ShapeDtypeStruct((nb, ns, d), F32),
        compiler_params=_cparams("parallel", "arbitrary"),
        name="moba_sample",
    )(page_table, q, k_new, v_new, *([pool_k] * pages_per_step), *([pool_v] * pages_per_step))


def _matmul_residual_kernel(a_ref, w_ref, r_ref, o_ref):
    o_ref[...] = r_ref[...] + jnp.dot(a_ref[...].astype(BF16), w_ref[...].astype(BF16),
                                      preferred_element_type=F32)


def _matmul_residual(a, w, res):
    n, k = a.shape
    m = w.shape[1]
    tm = min(TM_TOKENS, n)
    assert n % tm == 0
    return pl.pallas_call(
        _matmul_residual_kernel,
        grid=(n // tm,),
        in_specs=[pl.BlockSpec((tm, k), lambda i: (i, 0)),
                  pl.BlockSpec((k, m), lambda i: (0, 0)),
                  pl.BlockSpec((tm, m), lambda i: (i, 0))],
        out_specs=pl.BlockSpec((tm, m), lambda i: (i, 0)),
        out_shape=jax.ShapeDtypeStruct((n, m), F32),
        compiler_params=_cparams("parallel"),
        name="matmul_residual",
    )(a, w, res)


def _dense_ffn_kernel(x_ref, g_ref, wg_ref, wu_ref, wd_ref, o_ref, h_ref):
    @pl.when(pl.program_id(1) == 0)
    def _():
        x = x_ref[...]
        h_ref[...] = _rmsnorm_rows(x, g_ref[...]).astype(BF16)
        o_ref[...] = x
    h = h_ref[...]
    gate = jnp.dot(h, wg_ref[...].astype(BF16), preferred_element_type=F32)
    up = jnp.dot(h, wu_ref[...].astype(BF16), preferred_element_type=F32)
    act = (gate * jax.nn.sigmoid(gate)) * up
    o_ref[...] += jnp.dot(act.astype(BF16), wd_ref[...].astype(BF16), preferred_element_type=F32)


def _dense_ffn(x, g, w_gu, w_down):
    n, d = x.shape
    f = w_down.shape[0]
    tm, tf = min(TM_FFN, n), TF_FFN
    assert n % tm == 0 and f % tf == 0
    nf = f // tf
    return pl.pallas_call(
        _dense_ffn_kernel,
        grid=(n // tm, nf),
        in_specs=[pl.BlockSpec((tm, d), lambda i, c: (i, 0)),
                  pl.BlockSpec((1, d), lambda i, c: (0, 0)),
                  pl.BlockSpec((d, tf), lambda i, c: (0, c)),
                  pl.BlockSpec((d, tf), lambda i, c: (0, nf + c)),
                  pl.BlockSpec((tf, d), lambda i, c: (c, 0))],
        out_specs=pl.BlockSpec((tm, d), lambda i, c: (i, 0)),
        out_shape=jax.ShapeDtypeStruct((n, d), F32),
        scratch_shapes=[pltpu.VMEM((tm, d), BF16)],
        compiler_params=_cparams("parallel", "arbitrary"),
        name="dense_ffn",
    )(x, g.reshape(1, d), w_gu, w_gu, w_down)


def _merge_groups(outs, lses):
    m = functools.reduce(jnp.maximum, lses)
    ws = [jnp.exp(l - m) for l in lses]
    num = functools.reduce(lambda a, b: a + b, [w * o for w, o in zip(ws, outs)])
    return num / functools.reduce(lambda a, b: a + b, ws)


def _dilated_prompt_kernel(q0_ref, q1_ref, q2_ref, k_ref, v_ref, o_ref,
                           qd_ref, kd_ref, vt_ref, od_ref, ld_ref, og_ref, lg_ref, *, s):
    span = DIL_SPAN
    ntiles = s // span
    head_of_row = lax.broadcasted_iota(I32, (LANES, span), 0) // HEAD_DIM
    key_i = lax.broadcasted_iota(I32, (span, span), 0)
    qry_i = lax.broadcasted_iota(I32, (span, span), 1)

    for g, (q_ref, (_, dil)) in enumerate(zip((q0_ref, q1_ref, q2_ref), DIL_GROUPS)):
        sub = s // dil
        tiles_per_class = sub // span
        for r in range(dil):
            src = pl.ds(r, sub, stride=dil) if dil > 1 else pl.ds(0, s)
            dst = slice(r * sub, (r + 1) * sub)
            qd_ref[dst, :] = q_ref[src, :] * ATTN_SCALE
            kd_ref[dst, :] = k_ref[src, :].astype(BF16)
            vr = v_ref[src, :]
            for tt in range(tiles_per_class):
                vt_ref[r * tiles_per_class + tt] = vr[tt * span:(tt + 1) * span, :].T.astype(BF16)

        def tile(t, carry, tiles_per_class=tiles_per_class):
            t_prev = jnp.maximum(t - 1, 0)
            cur = pl.ds(pl.multiple_of(t * span, span), span)
            prev = pl.ds(pl.multiple_of(t_prev * span, span), span)
            prev_off = jnp.where((t % tiles_per_class) != 0, 0, span)
            q_t = qd_ref[cur, :].T
            k2 = jnp.concatenate([kd_ref[prev, :], kd_ref[cur, :]], axis=0)
            vt2 = jnp.concatenate([vt_ref[t_prev], vt_ref[t]], axis=1)
            outs, lses = [], []
            for h in range(HEADS_PER_VREG):
                qh = jnp.where(head_of_row == h, q_t, 0.0).astype(BF16)
                sa = jnp.dot(k2, qh, preferred_element_type=F32)
                sp = jnp.where(key_i >= qry_i + prev_off, sa[:span], NEG_INF)
                sc = jnp.where(key_i <= qry_i, sa[span:], NEG_INF)
                sa = jnp.concatenate([sp, sc], axis=0)
                m = jnp.max(sa, axis=0, keepdims=True)
                p = jnp.exp(sa - m)
                l = jnp.sum(p, axis=0, keepdims=True)
                dims = slice(h * HEAD_DIM, (h + 1) * HEAD_DIM)
                outs.append(jnp.dot(vt2[dims, :], p.astype(BF16), preferred_element_type=F32) / l)
                lses.append(jnp.broadcast_to(m + jnp.log(l), (HEAD_DIM, span)))
            od_ref[cur, :] = jnp.concatenate(outs, axis=0).T
            ld_ref[cur, :] = jnp.concatenate(lses, axis=0).T
            return carry

        lax.fori_loop(0, ntiles, tile, 0, unroll=8)
        for r in range(dil):
            dst = pl.ds(r, sub, stride=dil) if dil > 1 else pl.ds(0, s)
            src = slice(r * sub, (r + 1) * sub)
            og_ref[g, dst, :] = od_ref[src, :]
            lg_ref[g, dst, :] = ld_ref[src, :]

    ng = len(DIL_GROUPS)
    o_ref[...] = _merge_groups([og_ref[g] for g in range(ng)], [lg_ref[g] for g in range(ng)])


def _dilated_prompt(q, k, v):
    b, s, hd = k.shape
    ng = len(DIL_GROUPS)
    assert q.shape == (b, s, ng * hd) and hd % LANES == 0
    assert all(s % (DIL_SPAN * dil) == 0 and win // dil == DIL_SPAN for win, dil in DIL_GROUPS)
    hp_blocks = hd // LANES

    def q_spec(g):
        return pl.BlockSpec((None, s, LANES), lambda bi, hp: (bi, 0, g * hp_blocks + hp))

    kv_spec = pl.BlockSpec((None, s, LANES), lambda bi, hp: (bi, 0, hp))
    return pl.pallas_call(
        functools.partial(_dilated_prompt_kernel, s=s),
        grid=(b, hp_blocks),
        in_specs=[q_spec(0), q_spec(1), q_spec(2), kv_spec, kv_spec],
        out_specs=kv_spec,
        out_shape=jax.ShapeDtypeStruct((b, s, hd), F32),
        scratch_shapes=[pltpu.VMEM((s, LANES), F32), pltpu.VMEM((s, LANES), BF16),
                        pltpu.VMEM((s // DIL_SPAN, LANES, DIL_SPAN), BF16),
                        pltpu.VMEM((s, LANES), F32), pltpu.VMEM((s, LANES), F32),
                        pltpu.VMEM((ng, s, LANES), F32), pltpu.VMEM((ng, s, LANES), F32)],
        compiler_params=_cparams("parallel", "parallel"),
        name="dilated_prompt",
    )(q, q, q, k, v)


def _dilated_sample_kernel(q_ref, kn_ref, vn_ref, kt_ref, vt_ref, o_ref, *, ns, nh, w_buf):
    d = nh * HEAD_DIM
    ng = len(DIL_GROUPS)
    rows = ng * ns * nh
    contract_last = (((1,), (1,)), ((), ()))

    q = q_ref[...]
    head_of_lane = lax.broadcasted_iota(I32, (nh, d), 1) // HEAD_DIM
    head_of_row = lax.broadcasted_iota(I32, (nh, d), 0)
    pieces = []
    for g in range(ng):
        for qi in range(ns):
            qrow = jnp.broadcast_to(q[qi:qi + 1, g * d:(g + 1) * d], (nh, d))
            pieces.append(jnp.where(head_of_lane == head_of_row, qrow, 0.0))
    qf = jnp.concatenate(pieces, axis=0)
    qb = (qf * ATTN_SCALE).astype(BF16)

    kn, vn = kn_ref[...], vn_ref[...]
    s_new = [jnp.sum(qf * kn[j:j + 1, :], axis=1, keepdims=True) * ATTN_SCALE for j in range(ns)]

    r1 = lax.broadcasted_iota(I32, (rows, 1), 0)
    g1, q1 = r1 // (ns * nh), (r1 // nh) % ns
    win_r = jnp.zeros((rows, 1), I32)
    low_bits_r = jnp.zeros((rows, 1), I32)
    for g, (win, dil) in enumerate(DIL_GROUPS):
        win_r = jnp.where(g1 == g, win, win_r)
        low_bits_r = jnp.where(g1 == g, dil - 1, low_bits_r)

    s = jnp.dot(qb, kt_ref[...].astype(BF16), preferred_element_type=F32)
    dist = (w_buf + q1) - lax.broadcasted_iota(I32, (rows, w_buf), 1)
    s = jnp.where(((dist & low_bits_r) == 0) & (dist <= win_r), s, NEG_INF)
    s_new = [jnp.where((q1 >= j) & (((q1 - j) & low_bits_r) == 0) & (q1 - j <= win_r), s_new[j], NEG_INF)
             for j in range(ns)]
    m = functools.reduce(jnp.maximum, [jnp.max(s, axis=1, keepdims=True)] + s_new)
    p = jnp.exp(s - m)
    l = jnp.sum(p, axis=1, keepdims=True)
    acc = lax.dot_general(p.astype(BF16), vt_ref[...].astype(BF16), contract_last, preferred_element_type=F32)
    for j in range(ns):
        w = jnp.exp(s_new[j] - m)
        l = l + w
        acc = acc + w * vn[j:j + 1, :]
    o_all = acc / l
    lse_all = jnp.broadcast_to(m + jnp.log(l), o_all.shape)

    own = (lax.broadcasted_iota(I32, o_all.shape, 1) // HEAD_DIM) == (lax.broadcasted_iota(I32, o_all.shape, 0) % nh)
    o_gq = jnp.sum(jnp.where(own, o_all, 0.0).reshape(ng * ns, nh, d), axis=1)
    lse_gq = jnp.sum(jnp.where(own, lse_all, 0.0).reshape(ng * ns, nh, d), axis=1)
    o_ref[...] = _merge_groups([o_gq[g * ns:(g + 1) * ns] for g in range(ng)],
                               [lse_gq[g * ns:(g + 1) * ns] for g in range(ng)])


def _dilated_sample(q, k_new, v_new, cache_k, cache_v):
    nb, ns, d = k_new.shape
    _, w_buf, nh, _ = cache_k.shape
    ng = len(DIL_GROUPS)
    assert all(dil & (dil - 1) == 0 for _, dil in DIL_GROUPS)
    assert d == nh * HEAD_DIM and nh % 8 == 0 and w_buf % LANES == 0 and q.shape == (nb, ns, ng * d)
    transposed = lambda c: jnp.transpose(c, (0, 2, 3, 1)).reshape(nb, d, w_buf)
    new_spec = pl.BlockSpec((None, ns, d), lambda b: (b, 0, 0))
    buf_spec = pl.BlockSpec((None, d, w_buf), lambda b: (b, 0, 0))
    return pl.pallas_call(
        functools.partial(_dilated_sample_kernel, ns=ns, nh=nh, w_buf=w_buf),
        grid=(nb,),
        in_specs=[pl.BlockSpec((None, ns, ng * d), lambda b: (b, 0, 0)), new_spec, new_spec, buf_spec, buf_spec],
        out_specs=new_spec,
        out_shape=jax.ShapeDtypeStruct((nb, ns, d), F32),
        compiler_params=_cparams("parallel"),
        name="dilated_sample",
    )(q, k_new, v_new, transposed(cache_k), transposed(cache_v))


def _router_kernel(x_ref, g_ref, wr_ref, h_ref, route_ref):
    h = _rmsnorm_rows(x_ref[...], g_ref[...])
    h_ref[...] = h
    logits = jnp.dot(h, wr_ref[...], precision=HIGHEST, preferred_element_type=F32)
    lane = lax.broadcasted_iota(I32, logits.shape, 1)
    lane_f = lane.astype(F32)
    l1 = jnp.where(lane < N_EXPERTS, logits, NEG_INF)
    m1 = jnp.max(l1, axis=1, keepdims=True)
    i1 = jnp.min(jnp.where(l1 == m1, lane_f, float(LANES)), axis=1, keepdims=True)
    l2 = jnp.where(lane_f == i1, NEG_INF, l1)
    m2 = jnp.max(l2, axis=1, keepdims=True)
    i2 = jnp.min(jnp.where(l2 == m2, lane_f, float(LANES)), axis=1, keepdims=True)
    e = jnp.exp(m2 - m1)
    den = 1.0 + e
    packed = jnp.where(lane == 0, i1, jnp.where(lane == 1, i2,
             jnp.where(lane == 2, 1.0 / den, jnp.where(lane == 3, e / den, 0.0))))
    route_ref[...] = packed[:, :route_ref.shape[1]]


ROUTE_COLS = 8


def _router(x, g, w_router_padded):
    n, d = x.shape
    tm = min(TM_ROUTE, n)
    assert n % tm == 0
    return pl.pallas_call(
        _router_kernel,
        grid=(n // tm,),
        in_specs=[pl.BlockSpec((tm, d), lambda i: (i, 0)),
                  pl.BlockSpec((1, d), lambda i: (0, 0)),
                  pl.BlockSpec((d, LANES), lambda i: (0, 0))],
        out_specs=[pl.BlockSpec((tm, d), lambda i: (i, 0)),
                   pl.BlockSpec((tm, ROUTE_COLS), lambda i: (i, 0))],
        out_shape=[jax.ShapeDtypeStruct((n, d), F32), jax.ShapeDtypeStruct((n, ROUTE_COLS), F32)],
        compiler_params=_cparams("parallel"),
        name="moe_router",
    )(x, g.reshape(1, d), w_router_padded)


def _dispatch_kernel(dest_ref, h_ref, slots_in, slots_out, sem, *, tm):
    del slots_in

    def row_copy(r, k):
        return pltpu.make_async_copy(h_ref.at[pl.ds(r, 1)],
                                     slots_out.at[pl.ds(dest_ref[0, TOP_K * r + k], 1)], sem)

    def issue(r, c):
        for k in range(TOP_K):
            row_copy(r, k).start()
        return c

    def drain(r, c):
        for k in range(TOP_K):
            row_copy(r, k).wait()
        return c

    lax.fori_loop(0, tm, issue, 0, unroll=8)
    lax.fori_loop(0, tm, drain, 0, unroll=8)


def _dispatch(h, dest, slots):
    n, d = h.shape
    tm = min(TM_ROUTE, n)
    assert n % tm == 0
    return pl.pallas_call(
        functools.partial(_dispatch_kernel, tm=tm),
        grid=(n // tm,),
        in_specs=[pl.BlockSpec((None, 1, TOP_K * tm), lambda i: (i, 0, 0), memory_space=pltpu.SMEM),
                  pl.BlockSpec((tm, d), lambda i: (i, 0)),
                  pl.BlockSpec(memory_space=pl.ANY)],
        out_specs=pl.BlockSpec(memory_space=pl.ANY),
        out_shape=jax.ShapeDtypeStruct(slots.shape, slots.dtype),
        scratch_shapes=[pltpu.SemaphoreType.DMA(())],
        input_output_aliases={2: 0},
        compiler_params=_cparams("arbitrary"),
        name="moe_dispatch",
    )(dest.reshape(n // tm, 1, TOP_K * tm), h, slots)


def _moe_gu_kernel(exp_ref, valid_ref, first_ref, x_ref, wg_ref, wu_ref, o_ref, wgb_ref, wub_ref):
    del exp_ref
    t = pl.program_id(1)

    @pl.when(first_ref[t] == 1)
    def _():
        wgb_ref[...] = wg_ref[...].astype(BF16)
        wub_ref[...] = wu_ref[...].astype(BF16)

    @pl.when(valid_ref[t] == 1)
    def _():
        x = x_ref[...].astype(BF16)
        gate = jnp.dot(x, wgb_ref[...], preferred_element_type=F32)
        up = jnp.dot(x, wub_ref[...], preferred_element_type=F32)
        o_ref[...] = ((gate * jax.nn.sigmoid(gate)) * up).astype(BF16)

    @pl.when(valid_ref[t] == 0)
    def _():
        o_ref[...] = jnp.zeros(o_ref.shape, o_ref.dtype)


def _moe_gu(slots, w_gu, tile_exp, tile_valid, tile_first):
    r, d = slots.shape
    f = w_gu.shape[2] // 2
    tm, tn = TM_MOE, TN_MOE_GU
    assert r % tm == 0 and f % tn == 0
    nj = f // tn
    return pl.pallas_call(
        _moe_gu_kernel,
        grid_spec=pltpu.PrefetchScalarGridSpec(
            num_scalar_prefetch=3,
            grid=(nj, r // tm),
            in_specs=[pl.BlockSpec((tm, d), lambda j, t, ex, va, fi: (t, 0)),
                      pl.BlockSpec((None, d, tn), lambda j, t, ex, va, fi: (ex[t], 0, j)),
                      pl.BlockSpec((None, d, tn), lambda j, t, ex, va, fi: (ex[t], 0, nj + j))],
            out_specs=pl.BlockSpec((tm, tn), lambda j, t, ex, va, fi: (t, j)),
            scratch_shapes=[pltpu.VMEM((d, tn), BF16), pltpu.VMEM((d, tn), BF16)]),
        out_shape=jax.ShapeDtypeStruct((r, f), BF16),
        compiler_params=_cparams("arbitrary", "arbitrary"),
        name="moe_gate_up",
    )(tile_exp, tile_valid, tile_first, slots, w_gu, w_gu)


def _moe_down_kernel(exp_ref, valid_ref, first_ref, a_ref, wd_ref, o_ref, wdb_ref):
    del exp_ref
    t = pl.program_id(1)

    @pl.when(first_ref[t] == 1)
    def _():
        wdb_ref[...] = wd_ref[...].astype(BF16)

    @pl.when(valid_ref[t] == 1)
    def _():
        o_ref[...] = jnp.dot(a_ref[...], wdb_ref[...], preferred_element_type=F32)

    @pl.when(valid_ref[t] == 0)
    def _():
        o_ref[...] = jnp.zeros(o_ref.shape, o_ref.dtype)


def _moe_down(act, w_down, tile_exp, tile_valid, tile_first):
    r, f = act.shape
    d = w_down.shape[2]
    tm, tn = TM_MOE, TN_MOE_DOWN
    assert r % tm == 0 and d % tn == 0
    return pl.pallas_call(
        _moe_down_kernel,
        grid_spec=pltpu.PrefetchScalarGridSpec(
            num_scalar_prefetch=3,
            grid=(d // tn, r // tm),
            in_specs=[pl.BlockSpec((tm, f), lambda j, t, ex, va, fi: (t, 0)),
                      pl.BlockSpec((None, f, tn), lambda j, t, ex, va, fi: (ex[t], 0, j))],
            out_specs=pl.BlockSpec((tm, tn), lambda j, t, ex, va, fi: (t, j)),
            scratch_shapes=[pltpu.VMEM((f, tn), BF16)]),
        out_shape=jax.ShapeDtypeStruct((r, d), F32),
        compiler_params=_cparams("arbitrary", "arbitrary"),
        name="moe_down",
    )(tile_exp, tile_valid, tile_first, act, w_down)


def _combine_kernel(dest_ref, x_ref, route_ref, rows_hbm, y_ref, buf_ref, sem, *, tm):
    def row_copy(r, k):
        return pltpu.make_async_copy(rows_hbm.at[pl.ds(dest_ref[0, TOP_K * r + k], 1)],
                                     buf_ref.at[k, pl.ds(r, 1)], sem)

    def issue(r, c):
        for k in range(TOP_K):
            row_copy(r, k).start()
        return c

    def drain(r, c):
        for k in range(TOP_K):
            row_copy(r, k).wait()
        return c

    lax.fori_loop(0, tm, issue, 0, unroll=8)
    lax.fori_loop(0, tm, drain, 0, unroll=8)
    route = route_ref[...]
    y_ref[...] = x_ref[...] + (route[:, 2:3] * buf_ref[0] + route[:, 3:4] * buf_ref[1])


def _combine(x, route, dest, rows):
    n, d = x.shape
    tm = min(TM_ROUTE, n)
    assert n % tm == 0
    return pl.pallas_call(
        functools.partial(_combine_kernel, tm=tm),
        grid=(n // tm,),
        in_specs=[pl.BlockSpec((None, 1, TOP_K * tm), lambda i: (i, 0, 0), memory_space=pltpu.SMEM),
                  pl.BlockSpec((tm, d), lambda i: (i, 0)),
                  pl.BlockSpec((tm, ROUTE_COLS), lambda i: (i, 0)),
                  pl.BlockSpec(memory_space=pl.ANY)],
        out_specs=pl.BlockSpec((tm, d), lambda i: (i, 0)),
        out_shape=jax.ShapeDtypeStruct((n, d), F32),
        scratch_shapes=[pltpu.VMEM((TOP_K, tm, d), F32), pltpu.SemaphoreType.DMA(())],
        compiler_params=_cparams("arbitrary"),
        name="moe_combine",
    )(dest.reshape(n // tm, 1, TOP_K * tm), x, route, rows)


def _moe_ffn(xs, g, w_router, w_gu, w_down):
    d = xs[0].shape[1]
    wr = jnp.zeros((d, LANES), F32).at[:, :N_EXPERTS].set(w_router)
    routed = [_router(x, g, wr) for x in xs]
    e_flat = jnp.concatenate([route[:, :TOP_K] for _, route in routed], axis=0).astype(I32).reshape(-1)
    a = e_flat.shape[0]
    onehot = (e_flat[:, None] == jnp.arange(N_EXPERTS, dtype=I32)[None, :]).astype(I32)
    csum = jnp.cumsum(onehot, axis=0)
    rank = jnp.sum(csum * onehot, axis=1) - 1
    counts = csum[-1]
    padded = (counts + TM_MOE - 1) // TM_MOE * TM_MOE
    pends = jnp.cumsum(padded)
    dest = (jnp.sum((pends - padded)[None, :] * onehot, axis=1) + rank).astype(I32)
    n_tiles = -(-a // TM_MOE) + N_EXPERTS
    tile_start = jnp.arange(n_tiles, dtype=I32) * TM_MOE
    tile_exp = jnp.minimum(jnp.sum(tile_start[:, None] >= pends[None, :], axis=1), N_EXPERTS - 1).astype(I32)
    tile_valid = (tile_start < pends[-1]).astype(I32)
    tile_first = jnp.concatenate([jnp.ones((1,), I32), (tile_exp[1:] != tile_exp[:-1]).astype(I32)])

    slots = jnp.zeros((n_tiles * TM_MOE, d), F32)
    dests, off = [], 0
    for h, _ in routed:
        dests.append(dest[off:off + TOP_K * h.shape[0]])
        slots = _dispatch(h, dests[-1], slots)
        off += TOP_K * h.shape[0]
    act = _moe_gu(slots, w_gu, tile_exp, tile_valid, tile_first)
    rows = _moe_down(act, w_down, tile_exp, tile_valid, tile_first)
    return [_combine(x, route, dst, rows) for x, (_, route), dst in zip(xs, routed, dests)]


def kernel(x_prompt, x_sample, cache_a_k, cache_a_v, cache_b_k, cache_b_v, page_table, norm_attn_a, w_qkv_a,
           qnorm_a, knorm_a, w_o_a, norm_kv_b, w_kv_b, knorm_b, norm_attn_b, w_q_b, qnorm_b, w_o_b, norm_ffn,
           w_gu_dense, w_down_dense, w_router, w_gu_moe, w_down_moe):
    b, s, d = x_prompt.shape
    nb, ns, _ = x_sample.shape
    depth, n_a = norm_ffn.shape[0], w_qkv_a.shape[0]
    n_phys, page = cache_a_k.shape[1], cache_a_k.shape[2]
    past = page_table.shape[1] * page
    heads_a = d // HEAD_DIM
    d_b = w_kv_b.shape[1] // 2
    heads_b = d_b // HEAD_DIM
    n_groups = len(DIL_GROUPS)
    w_buf = cache_b_k.shape[1]
    w_max = max(w for w, _ in DIL_GROUPS)

    xp = x_prompt.reshape(b * s, d)
    xs = x_sample.reshape(nb * ns, d)
    tabs_p = _rope_tables(jnp.tile(jnp.arange(s), b))
    tabs_s = _rope_tables(jnp.tile(past + jnp.arange(ns), nb))
    ones_d = jnp.ones((d,), F32)
    ak_p, av_p, ak_s, av_s = [], [], [], []
    for l in range(depth):
        if l < n_a:
            gains = jnp.concatenate([jnp.tile(qnorm_a[l], heads_a), jnp.tile(knorm_a[l], heads_a), ones_d])
            qp, kp, kp_t, vp, vp_t = _norm_proj(xp, norm_attn_a[l], w_qkv_a[l], gains, tabs_p,
                                                ((d, True, False), (d, True, True), (d, False, True)), seq=s)
            qs, ks, vs = _norm_proj(xs, norm_attn_a[l], w_qkv_a[l], gains, tabs_s,
                                    ((d, True, False), (d, True, False), (d, False, False)))
            op = _moba_prompt(qp.reshape(b, s, d), kp.reshape(b, s, d), vp.reshape(b, s, d))
            os_ = _moba_sample(qs.reshape(nb, ns, d), ks.reshape(nb, ns, d), vs.reshape(nb, ns, d),
                               cache_a_k[l], cache_a_v[l], page_table)
            xp = _matmul_residual(op.reshape(b * s, d), w_o_a[l], xp)
            xs = _matmul_residual(os_.reshape(nb * ns, d), w_o_a[l], xs)
            ak_p.append(kp_t)
            av_p.append(vp_t)
            ak_s.append(ks)
            av_s.append(vs)
        else:
            j = l - n_a
            gains = jnp.tile(qnorm_b[j], n_groups * heads_b)
            parts = ((n_groups * d_b, True, False),)
            (qp,) = _norm_proj(xp, norm_attn_b[j], w_q_b[j], gains, tabs_p, parts)
            (qs,) = _norm_proj(xs, norm_attn_b[j], w_q_b[j], gains, tabs_s, parts)
            op = _dilated_prompt(qp.reshape(b, s, n_groups * d_b), kb_p.reshape(b, s, d_b), vb_p.reshape(b, s, d_b))
            os_ = _dilated_sample(qs.reshape(nb, ns, n_groups * d_b), kb_s.reshape(nb, ns, d_b),
                                  vb_s.reshape(nb, ns, d_b), cache_b_k, cache_b_v)
            xp = _matmul_residual(op.reshape(b * s, d_b), w_o_b[j], xp)
            xs = _matmul_residual(os_.reshape(nb * ns, d_b), w_o_b[j], xs)
        if l % 2 == 0:
            xp = _dense_ffn(xp, norm_ffn[l], w_gu_dense[l // 2], w_down_dense[l // 2])
            xs = _dense_ffn(xs, norm_ffn[l], w_gu_dense[l // 2], w_down_dense[l // 2])
        else:
            xp, xs = _moe_ffn([xp, xs], norm_ffn[l], w_router[l // 2], w_gu_moe[l // 2], w_down_moe[l // 2])
        if l == n_a - 1:
            gains = jnp.concatenate([jnp.tile(knorm_b, heads_b), jnp.ones((d_b,), F32)])
            kb_p, kb_pt, vb_p, vb_pt = _norm_proj(xp, norm_kv_b, w_kv_b, gains, tabs_p,
                                                  ((d_b, True, True), (d_b, False, True)), seq=s)
            kb_s, vb_s = _norm_proj(xs, norm_kv_b, w_kv_b, gains, tabs_s, ((d_b, True, False), (d_b, False, False)))

    keep = min(w_max, s)
    heads = lambda t, n, h: t.reshape(n_a, *n, h, HEAD_DIM)
    seq_major = lambda t, h: jnp.moveaxis(t.reshape(*t.shape[:-2], h, HEAD_DIM, t.shape[-1]), -1, -3)
    return (xp.reshape(b, s, d), xs.reshape(nb, ns, d),
            seq_major(jnp.stack(ak_p), heads_a), seq_major(jnp.stack(av_p), heads_a),
            seq_major(kb_pt, heads_b)[:, s - keep:], seq_major(vb_pt, heads_b)[:, s - keep:],
            heads(jnp.stack(ak_s), (nb, ns), heads_a), heads(jnp.stack(av_s), (nb, ns), heads_a),
            kb_s.reshape(nb, ns, heads_b, HEAD_DIM), vb_s.reshape(nb, ns, heads_b, HEAD_DIM))
```

```python
import functools

import jax
import jax.numpy as jnp
from jax import lax
from jax.experimental import pallas as pl
from jax.experimental.pallas import tpu as pltpu

F32 = jnp.float32
BF16 = jnp.bfloat16
I32 = jnp.int32
HIGHEST = lax.Precision.HIGHEST
NEG_INF = float("-inf")

HEAD_DIM = 64
MOBA_BLOCK = 256
MOBA_TOPK = 3
DIL_GROUPS = ((128, 1), (512, 4), (2048, 16))
DIL_SPAN = 128
ROT_DIM = HEAD_DIM // 4
ROPE_THETA = 500000.0
ATTN_SCALE = HEAD_DIM ** -0.5
RMS_EPS = 1e-6
N_EXPERTS = 8
TOP_K = 2

LANES = 128
HEADS_PER_VREG = LANES // HEAD_DIM
VMEM_LIMIT_BYTES = 56 * 1024 * 1024

TM_TOKENS = 512
TM_PROJ = 1024
TN_PROJ = 512
MOBA_SAMPLE_BLOCKS_PER_STEP = 8
TM_FFN = 1024
TF_FFN = 256
TM_MOE = 512
TN_MOE_GU = 1792
TN_MOE_DOWN = 1024
TM_ROUTE = 512


def _cparams(*sem):
    return pltpu.CompilerParams(dimension_semantics=sem, vmem_limit_bytes=VMEM_LIMIT_BYTES)


def _rmsnorm_rows(x, g):
    ms = jnp.mean(x * x, axis=-1, keepdims=True)
    return x * lax.rsqrt(ms + RMS_EPS) * g


MXU_COLS = 256


def _norm_proj_kernel(x_ref, g_ref, w_ref, hg_ref, c_ref, s1_ref, s2_ref, bd_ref, *rest, parts, tn):
    h_ref = rest[-1]
    out_refs = iter(rest[:-1])
    j = pl.program_id(1)

    @pl.when(j == 0)
    def _():
        h_ref[...] = _rmsnorm_rows(x_ref[...], g_ref[...]).astype(BF16)

    def head_norm_rope(y, cols):
        ss = jnp.dot((y * y).astype(BF16), bd_ref[...], preferred_element_type=F32)
        yn = y * lax.rsqrt(ss * (1.0 / HEAD_DIM) + RMS_EPS) * hg_ref[:, cols]
        c, s1, s2 = c_ref[...], s1_ref[...], s2_ref[...]
        halves = []
        for hb in range(MXU_COLS // LANES):
            yh = yn[:, hb * LANES:(hb + 1) * LANES]
            up = pltpu.roll(yh, LANES - ROT_DIM // 2, 1)
            dn = pltpu.roll(yh, ROT_DIM // 2, 1)
            halves.append(yh * c + up * s1 + dn * s2)
        return jnp.concatenate(halves, axis=1)

    for start, ntiles, normed, transposed in parts:
        o_ref = next(out_refs)
        ot_ref = next(out_refs) if transposed else None

        @pl.when((j >= start) & (j < start + ntiles))
        def _(o_ref=o_ref, ot_ref=ot_ref, normed=normed):
            h = h_ref[...]
            for pb in range(tn // MXU_COLS):
                cols = slice(pb * MXU_COLS, (pb + 1) * MXU_COLS)
                y = jnp.dot(h, w_ref[:, cols].astype(BF16), preferred_element_type=F32)
                if normed:
                    y = head_norm_rope(y, cols)
                o_ref[:, cols] = y
                if ot_ref is not None:
                    ot_ref[cols, :] = y.T


def _norm_proj(x, g, w, head_gain, rope_tabs, parts, seq=None):
    n, d = x.shape
    m = w.shape[1]
    tm, tn = min(TM_PROJ, n), TN_PROJ
    assert n % tm == 0 and m % tn == 0 and tn % MXU_COLS == 0 and all(p[0] % tn == 0 for p in parts)
    tile_parts, start = [], 0
    for width, normed, transposed in parts:
        tile_parts.append((start, width // tn, normed, transposed))
        start += width // tn
    c, s1, s2 = rope_tabs
    bd = (jnp.arange(MXU_COLS)[:, None] // HEAD_DIM == jnp.arange(MXU_COLS)[None, :] // HEAD_DIM).astype(BF16)

    out_specs, out_shape = [], []
    for (st, nt, _, transposed), (width, _, _) in zip(tile_parts, parts):
        col = lambda j, st=st, nt=nt: jnp.clip(j - st, 0, nt - 1)
        out_specs.append(pl.BlockSpec((tm, tn), lambda i, j, col=col: (i, col(j))))
        out_shape.append(jax.ShapeDtypeStruct((n, width), F32))
        if transposed:
            assert seq is not None and seq % tm == 0 and n % seq == 0
            per_seq = seq // tm
            out_specs.append(pl.BlockSpec((None, tn, tm),
                                          lambda i, j, col=col: (i // per_seq, col(j), i % per_seq)))
            out_shape.append(jax.ShapeDtypeStruct((n // seq, width, seq), F32))

    tab_spec = pl.BlockSpec((tm, LANES), lambda i, j: (i, 0))
    return pl.pallas_call(
        functools.partial(_norm_proj_kernel, parts=tuple(tile_parts), tn=tn),
        grid=(n // tm, m // tn),
        in_specs=[
            pl.BlockSpec((tm, d), lambda i, j: (i, 0)),
            pl.BlockSpec((1, d), lambda i, j: (0, 0)),
            pl.BlockSpec((d, tn), lambda i, j: (0, j)),
            pl.BlockSpec((1, tn), lambda i, j: (0, j)),
            tab_spec, tab_spec, tab_spec,
            pl.BlockSpec((MXU_COLS, MXU_COLS), lambda i, j: (0, 0)),
        ],
        out_specs=out_specs,
        out_shape=out_shape,
        scratch_shapes=[pltpu.VMEM((tm, d), BF16)],
        compiler_params=_cparams("parallel", "arbitrary"),
        name="norm_proj",
    )(x, g.reshape(1, d), w, head_gain.reshape(1, m), c, s1, s2, bd)


def _rope_tables(pos):
    half = ROT_DIM // 2
    inv = ROPE_THETA ** (-2.0 * jnp.arange(half, dtype=F32) / ROT_DIM)
    ang = pos.astype(F32)[:, None] * inv[None, :]
    cos, sin = jnp.cos(ang), jnp.sin(ang)
    n = pos.shape[0]
    zeros = jnp.zeros((n, HEAD_DIM - ROT_DIM), F32)
    zh = jnp.zeros((n, half), F32)
    c = jnp.concatenate([cos, cos, jnp.ones((n, HEAD_DIM - ROT_DIM), F32)], axis=1)
    s1 = jnp.concatenate([-sin, zh, zeros], axis=1)
    s2 = jnp.concatenate([zh, sin, zeros], axis=1)
    tile = lambda t: jnp.tile(t, (1, HEADS_PER_VREG))
    return tile(c), tile(s1), tile(s2)


def _topk_block_mask(gate_t, n_valid):
    nblk = gate_t.shape[0]
    blk_idx = lax.broadcasted_iota(I32, gate_t.shape, 0)
    rank = jnp.zeros(gate_t.shape, I32)
    for m in range(nblk):
        gm = gate_t[m:m + 1, :]
        beats = (gm > gate_t) | ((gm == gate_t) & (m < blk_idx))
        rank = rank + jnp.where(beats & (m < n_valid), 1, 0)
    return jnp.where((blk_idx < n_valid) & (rank < MOBA_TOPK), 1.0, 0.0)


def _moba_prompt_kernel(q_ref, k_ref, v_ref, o_ref, kb_ref, vt_ref, vt1_ref, km_ref, *, nblk, nh_step):
    qi = pl.program_id(2)
    blk = MOBA_BLOCK

    @pl.when(qi == 0)
    def _():
        for n in range(nblk):
            rows = slice(n * blk, (n + 1) * blk)
            half = slice((n % 2) * blk, (n % 2 + 1) * blk)
            kn = k_ref[rows, :]
            kb_ref[n // 2, half, :] = kn.astype(BF16)
            vn_t = v_ref[rows, :].T.astype(BF16)
            vt_ref[n // 2, :, half] = vn_t
            vt1_ref[n] = vn_t
            km_ref[n:n + 1, :] = jnp.mean(kn, axis=0, keepdims=True)

    q_t = q_ref[...].T
    head_of_row = lax.broadcasted_iota(I32, (nh_step * HEAD_DIM, blk), 0) // HEAD_DIM
    causal = lax.broadcasted_iota(I32, (blk, blk), 0) <= lax.broadcasted_iota(I32, (blk, blk), 1)
    blk_idx = lax.broadcasted_iota(I32, (nblk, blk), 0)
    km = km_ref[...]
    k_own = kb_ref[qi // 2, pl.ds(pl.multiple_of((qi % 2) * blk, blk), blk), :]
    vt_own = vt1_ref[qi]

    qs, sel_t, init = [], [], []
    for h in range(nh_step):
        qh = jnp.where(head_of_row == h, q_t, 0.0)
        gate_t = jnp.dot(km, qh, precision=HIGHEST, preferred_element_type=F32)
        sel_t.append(_topk_block_mask(gate_t, qi))
        qs.append((qh * ATTN_SCALE).astype(BF16))
        s = jnp.where(causal, jnp.dot(k_own, qs[h], preferred_element_type=F32), NEG_INF)
        m = jnp.max(s, axis=0, keepdims=True)
        p = jnp.exp(s - m)
        dims = slice(h * HEAD_DIM, (h + 1) * HEAD_DIM)
        init.append((m, jnp.sum(p, axis=0, keepdims=True),
                     jnp.dot(vt_own[dims, :], p.astype(BF16), preferred_element_type=F32)))

    def block_pair(pi, carry):
        k2 = kb_ref[pi]
        vt2 = vt_ref[pi]
        new = []
        for h in range(nh_step):
            m, l, a = carry[h]
            s = jnp.dot(k2, qs[h], preferred_element_type=F32)
            vis = [jnp.max(jnp.where(blk_idx == 2 * pi + c, sel_t[h], 0.0), axis=0, keepdims=True) > 0.0
                   for c in range(2)]
            col_max = [jnp.max(s[c * blk:(c + 1) * blk], axis=0, keepdims=True) for c in range(2)]
            m_new = functools.reduce(jnp.maximum, [m] + [jnp.where(v, cm, NEG_INF) for v, cm in zip(vis, col_max)])
            alpha = jnp.exp(m - m_new)
            p = jnp.concatenate([jnp.exp(s[c * blk:(c + 1) * blk] - jnp.where(vis[c], m_new, jnp.inf))
                                 for c in range(2)], axis=0)
            l = alpha * l + jnp.sum(p, axis=0, keepdims=True)
            dims = slice(h * HEAD_DIM, (h + 1) * HEAD_DIM)
            a = alpha * a + jnp.dot(vt2[dims, :], p.astype(BF16), preferred_element_type=F32)
            new.append((m_new, l, a))
        return tuple(new)

    final = lax.fori_loop(0, (qi + 1) // 2, block_pair, tuple(init))
    o_ref[...] = jnp.concatenate([a / l for _, l, a in final], axis=0).T


MOBA_HEADS_PER_STEP = 4


def _moba_prompt(q, k, v):
    b, s, hd = q.shape
    nblk = s // MOBA_BLOCK
    w = MOBA_HEADS_PER_STEP * HEAD_DIM
    assert s % (2 * MOBA_BLOCK) == 0 and hd % w == 0 and w % LANES == 0
    q_spec = pl.BlockSpec((None, MOBA_BLOCK, w), lambda bi, hp, qi: (bi, qi, hp))
    kv_spec = pl.BlockSpec((None, s, w), lambda bi, hp, qi: (bi, 0, hp))
    return pl.pallas_call(
        functools.partial(_moba_prompt_kernel, nblk=nblk, nh_step=MOBA_HEADS_PER_STEP),
        grid=(b, hd // w, nblk),
        in_specs=[q_spec, kv_spec, kv_spec],
        out_specs=q_spec,
        out_shape=jax.ShapeDtypeStruct((b, s, hd), F32),
        scratch_shapes=[pltpu.VMEM((nblk // 2, 2 * MOBA_BLOCK, w), BF16),
                        pltpu.VMEM((nblk // 2, w, 2 * MOBA_BLOCK), BF16),
                        pltpu.VMEM((nblk, w, MOBA_BLOCK), BF16),
                        pltpu.VMEM((nblk, w), F32)],
        compiler_params=_cparams("parallel", "parallel", "arbitrary"),
        name="moba_prompt",
    )(q, k, v)


def _moba_sample_kernel(pt_ref, q_ref, kn_ref, vn_ref, *rest, nblk, ns, nh, blocks_per_step):
    del pt_ref
    n_pages = 2 * blocks_per_step
    k_pages, v_pages = rest[:n_pages], rest[n_pages:2 * n_pages]
    o_ref, qf_ref, q2_ref, g_ref, m_ref, l_ref, acc_ref = rest[2 * n_pages:]
    step = pl.program_id(1)
    n = step * blocks_per_step
    rows = ns * nh
    d = nh * HEAD_DIM
    page = MOBA_BLOCK // 2
    contract_last = (((1,), (1,)), ((), ()))

    @pl.when(step == 0)
    def _():
        q = q_ref[...]
        head_of_lane = lax.broadcasted_iota(I32, (nh, d), 1) // HEAD_DIM
        head_of_row = lax.broadcasted_iota(I32, (nh, d), 0)
        for qi in range(ns):
            piece = jnp.where(head_of_lane == head_of_row, jnp.broadcast_to(q[qi:qi + 1, :], (nh, d)), 0.0)
            hi = piece.astype(BF16)
            qf_ref[qi * nh:(qi + 1) * nh, :] = piece
            q2_ref[qi * nh:(qi + 1) * nh, :] = (piece * ATTN_SCALE).astype(BF16)
            q2_ref[rows + qi * nh:rows + (qi + 1) * nh, :] = (piece - hi.astype(F32)).astype(BF16)

    q2 = q2_ref[...]

    def page_scores(k_page_ref):
        k = k_page_ref[...]
        k_hi = k.astype(BF16)
        k_lo = (k - k_hi.astype(F32)).astype(BF16)
        s2 = jnp.dot(q2, k_hi, preferred_element_type=F32)
        s_lo = jnp.dot(q2[:rows], k_lo, preferred_element_type=F32)
        return s2[:rows], (s2[:rows] + s_lo) * (1.0 / ATTN_SCALE) + s2[rows:]

    for c in range(blocks_per_step):
        sa, ga = page_scores(k_pages[2 * c])
        sb, gb = page_scores(k_pages[2 * c + 1])
        s = jnp.concatenate([sa, sb], axis=1)
        gate = (jnp.sum(ga, axis=1, keepdims=True) + jnp.sum(gb, axis=1, keepdims=True)) * (1.0 / MOBA_BLOCK)
        m = jnp.max(s, axis=1, keepdims=True)
        p = jnp.exp(s - m)
        l = jnp.sum(p, axis=1, keepdims=True)
        pb = p.astype(BF16)
        g_ref[n + c] = jnp.broadcast_to(gate, (rows, LANES))
        m_ref[n + c] = jnp.broadcast_to(m, (rows, LANES))
        l_ref[n + c] = jnp.broadcast_to(l, (rows, LANES))
        acc_ref[n + c] = (
            lax.dot_general(pb[:, :page], v_pages[2 * c][...].astype(BF16), contract_last, preferred_element_type=F32)
            + lax.dot_general(pb[:, page:], v_pages[2 * c + 1][...].astype(BF16), contract_last,
                              preferred_element_type=F32))

    @pl.when(step == nblk // blocks_per_step - 1)
    def _():
        sel = []
        for a in range(nblk):
            ga = g_ref[a]
            rank = jnp.zeros((rows, LANES), I32)
            for b in range(nblk):
                gb = g_ref[b]
                beats = (gb > ga) if b > a else (gb >= ga)
                if b != a:
                    rank = rank + jnp.where(beats, 1, 0)
            sel.append(rank < MOBA_TOPK)
        qf = qf_ref[...]
        kn, vn = kn_ref[...], vn_ref[...]
        q_of_row = lax.broadcasted_iota(I32, (rows, 1), 0) // nh
        s_own = [jnp.sum(qf * kn[j:j + 1, :], axis=1, keepdims=True) * ATTN_SCALE for j in range(ns)]
        s_own = [jnp.where(j <= q_of_row, s_own[j], NEG_INF) for j in range(ns)]
        m_blk = [jnp.where(sel[a], m_ref[a], NEG_INF)[:, 0:1] for a in range(nblk)]
        m_all = functools.reduce(jnp.maximum, s_own + m_blk)
        den = jnp.zeros((rows, 1), F32)
        num = jnp.zeros((rows, d), F32)
        for j in range(ns):
            w = jnp.exp(s_own[j] - m_all)
            den = den + w
            num = num + w * vn[j:j + 1, :]
        for a in range(nblk):
            w = jnp.exp(m_blk[a] - m_all)
            den = den + w * l_ref[a][:, 0:1]
            num = num + w * acc_ref[a]
        res = num / den
        head_of_lane = lax.broadcasted_iota(I32, (rows, d), 1) // HEAD_DIM
        head_of_row = lax.broadcasted_iota(I32, (rows, d), 0) % nh
        res = jnp.where(head_of_lane == head_of_row, res, 0.0)
        o_ref[...] = jnp.sum(res.reshape(ns, nh, d), axis=1)


def _moba_sample(q, k_new, v_new, pool_k, pool_v, page_table):
    nb, ns, d = q.shape
    n_phys, page, nh, _ = pool_k.shape
    nblk = page_table.shape[1] * page // MOBA_BLOCK
    assert 2 * page == MOBA_BLOCK and page_table.shape[1] * page == nblk * MOBA_BLOCK
    assert ns <= MOBA_BLOCK and nh % 8 == 0 and d == nh * HEAD_DIM
    rows = ns * nh
    pool_k = jnp.transpose(pool_k, (0, 2, 3, 1)).reshape(n_phys, d, page)
    pool_v = jnp.transpose(pool_v, (0, 2, 3, 1)).reshape(n_phys, d, page)
    new_spec = pl.BlockSpec((None, ns, d), lambda b, n, pt: (b, 0, 0))
    bps = MOBA_SAMPLE_BLOCKS_PER_STEP
    assert nblk % bps == 0
    pages_per_step = 2 * bps

    def page_spec(i):
        return pl.BlockSpec((None, d, page), lambda b, n, pt: (pt[b, pages_per_step * n + i], 0, 0))

    page_specs = [page_spec(i) for i in range(pages_per_step)]
    return pl.pallas_call(
        functools.partial(_moba_sample_kernel, nblk=nblk, ns=ns, nh=nh, blocks_per_step=bps),
        grid_spec=pltpu.PrefetchScalarGridSpec(
            num_scalar_prefetch=1,
            grid=(nb, nblk // bps),
            in_specs=[new_spec, new_spec, new_spec] + page_specs + page_specs,
            out_specs=new_spec,
            scratch_shapes=[
                pltpu.VMEM((rows, d), F32), pltpu.VMEM((2 * rows, d), BF16),
                pltpu.VMEM((nblk, rows, LANES), F32), pltpu.VMEM((nblk, rows, LANES), F32),
                pltpu.VMEM((nblk, rows, LANES), F32), pltpu.VMEM((nblk, rows, d), F32),
            ]),
        out_shape=jax.ShapeDtypeStruct((nb, ns, d), F32),
        compiler_params=_cparams("parallel", "arbitrary"),
        name="moba_sample",
    )(page_table, q, k_new, v_new, *([pool_k] * pages_per_step), *([pool_v] * pages_per_step))


def _matmul_residual_kernel(a_ref, w_ref, r_ref, o_ref):
    o_ref[...] = r_ref[...] + jnp.dot(a_ref[...].astype(BF16), w_ref[...].astype(BF16),
                                      preferred_element_type=F32)


def _matmul_residual(a, w, res):
    n, k = a.shape
    m = w.shape[1]
    tm = min(TM_TOKENS, n)
    assert n % tm == 0
    return pl.pallas_call(
        _matmul_residual_kernel,
        grid=(n // tm,),
        in_specs=[pl.BlockSpec((tm, k), lambda i: (i, 0)),
                  pl.BlockSpec((k, m), lambda i: (0, 0)),
                  pl.BlockSpec((tm, m), lambda i: (i, 0))],
        out_specs=pl.BlockSpec((tm, m), lambda i: (i, 0)),
        out_shape=jax.ShapeDtypeStruct((n, m), F32),
        compiler_params=_cparams("parallel"),
        name="matmul_residual",
    )(a, w, res)


def _dense_ffn_kernel(x_ref, g_ref, wg_ref, wu_ref, wd_ref, o_ref, h_ref):
    @pl.when(pl.program_id(1) == 0)
    def _():
        x = x_ref[...]
        h_ref[...] = _rmsnorm_rows(x, g_ref[...]).astype(BF16)
        o_ref[...] = x
    h = h_ref[...]
    gate = jnp.dot(h, wg_ref[...].astype(BF16), preferred_element_type=F32)
    up = jnp.dot(h, wu_ref[...].astype(BF16), preferred_element_type=F32)
    act = (gate * jax.nn.sigmoid(gate)) * up
    o_ref[...] += jnp.dot(act.astype(BF16), wd_ref[...].astype(BF16), preferred_element_type=F32)


def _dense_ffn(x, g, w_gu, w_down):
    n, d = x.shape
    f = w_down.shape[0]
    tm, tf = min(TM_FFN, n), TF_FFN
    assert n % tm == 0 and f % tf == 0
    nf = f // tf
    return pl.pallas_call(
        _dense_ffn_kernel,
        grid=(n // tm, nf),
        in_specs=[pl.BlockSpec((tm, d), lambda i, c: (i, 0)),
                  pl.BlockSpec((1, d), lambda i, c: (0, 0)),
                  pl.BlockSpec((d, tf), lambda i, c: (0, c)),
                  pl.BlockSpec((d, tf), lambda i, c: (0, nf + c)),
                  pl.BlockSpec((tf, d), lambda i, c: (c, 0))],
        out_specs=pl.BlockSpec((tm, d), lambda i, c: (i, 0)),
        out_shape=jax.ShapeDtypeStruct((n, d), F32),
        scratch_shapes=[pltpu.VMEM((tm, d), BF16)],
        compiler_params=_cparams("parallel", "arbitrary"),
        name="dense_ffn",
    )(x, g.reshape(1, d), w_gu, w_gu, w_down)


def _merge_groups(outs, lses):
    m = functools.reduce(jnp.maximum, lses)
    ws = [jnp.exp(l - m) for l in lses]
    num = functools.reduce(lambda a, b: a + b, [w * o for w, o in zip(ws, outs)])
    return num / functools.reduce(lambda a, b: a + b, ws)


def _dilated_prompt_kernel(q0_ref, q1_ref, q2_ref, k_ref, v_ref, o_ref,
                           qd_ref, kd_ref, vt_ref, od_ref, ld_ref, og_ref, lg_ref, *, s):
    span = DIL_SPAN
    ntiles = s // span
    head_of_row = lax.broadcasted_iota(I32, (LANES, span), 0) // HEAD_DIM
    key_i = lax.broadcasted_iota(I32, (span, span), 0)
    qry_i = lax.broadcasted_iota(I32, (span, span), 1)

    for g, (q_ref, (_, dil)) in enumerate(zip((q0_ref, q1_ref, q2_ref), DIL_GROUPS)):
        sub = s // dil
        tiles_per_class = sub // span
        for r in range(dil):
            src = pl.ds(r, sub, stride=dil) if dil > 1 else pl.ds(0, s)
            dst = slice(r * sub, (r + 1) * sub)
            qd_ref[dst, :] = q_ref[src, :] * ATTN_SCALE
            kd_ref[dst, :] = k_ref[src, :].astype(BF16)
            vr = v_ref[src, :]
            for tt in range(tiles_per_class):
                vt_ref[r * tiles_per_class + tt] = vr[tt * span:(tt + 1) * span, :].T.astype(BF16)

        for t in range(ntiles):
            has_prev = t % tiles_per_class != 0
            cur = slice(t * span, (t + 1) * span)
            q_t = qd_ref[cur, :].T
            if has_prev:
                k2 = kd_ref[(t - 1) * span:(t + 1) * span, :]
                vt2 = jnp.concatenate([vt_ref[t - 1], vt_ref[t]], axis=1)
            else:
                k2, vt2 = kd_ref[cur, :], vt_ref[t]
            outs, lses = [], []
            for h in range(HEADS_PER_VREG):
                qh = jnp.where(head_of_row == h, q_t, 0.0).astype(BF16)
                sa = jnp.dot(k2, qh, preferred_element_type=F32)
                if has_prev:
                    sa = jnp.concatenate([jnp.where(key_i >= qry_i, sa[:span], NEG_INF),
                                          jnp.where(key_i <= qry_i, sa[span:], NEG_INF)], axis=0)
                else:
                    sa = jnp.where(key_i <= qry_i, sa, NEG_INF)
                m = jnp.max(sa, axis=0, keepdims=True)
                p = jnp.exp(sa - m)
                l = jnp.sum(p, axis=0, keepdims=True)
                dims = slice(h * HEAD_DIM, (h + 1) * HEAD_DIM)
                outs.append(jnp.dot(vt2[dims, :], p.astype(BF16), preferred_element_type=F32) / l)
                lses.append(jnp.broadcast_to(m + jnp.log(l), (HEAD_DIM, span)))
            od_ref[cur, :] = jnp.concatenate(outs, axis=0).T
            ld_ref[cur, :] = jnp.concatenate(lses, axis=0).T
        for r in range(dil):
            dst = pl.ds(r, sub, stride=dil) if dil > 1 else pl.ds(0, s)
            src = slice(r * sub, (r + 1) * sub)
            og_ref[g, dst, :] = od_ref[src, :]
            lg_ref[g, dst, :] = ld_ref[src, :]

    ng = len(DIL_GROUPS)
    o_ref[...] = _merge_groups([og_ref[g] for g in range(ng)], [lg_ref[g] for g in range(ng)])


def _dilated_prompt(q, k, v):
    b, s, hd = k.shape
    ng = len(DIL_GROUPS)
    assert q.shape == (b, s, ng * hd) and hd % LANES == 0
    assert all(s % (DIL_SPAN * dil) == 0 and win // dil == DIL_SPAN for win, dil in DIL_GROUPS)
    hp_blocks = hd // LANES

    def q_spec(g):
        return pl.BlockSpec((None, s, LANES), lambda bi, hp: (bi, 0, g * hp_blocks + hp))

    kv_spec = pl.BlockSpec((None, s, LANES), lambda bi, hp: (bi, 0, hp))
    return pl.pallas_call(
        functools.partial(_dilated_prompt_kernel, s=s),
        grid=(b, hp_blocks),
        in_specs=[q_spec(0), q_spec(1), q_spec(2), kv_spec, kv_spec],
        out_specs=kv_spec,
        out_shape=jax.ShapeDtypeStruct((b, s, hd), F32),
        scratch_shapes=[pltpu.VMEM((s, LANES), F32), pltpu.VMEM((s, LANES), BF16),
                        pltpu.VMEM((s // DIL_SPAN, LANES, DIL_SPAN), BF16),
                        pltpu.VMEM((s, LANES), F32), pltpu.VMEM((s, LANES), F32),
                        pltpu.VMEM((ng, s, LANES), F32), pltpu.VMEM((ng, s, LANES), F32)],
        compiler_params=_cparams("parallel", "parallel"),
        name="dilated_prompt",
    )(q, q, q, k, v)


def _dilated_sample_kernel(q_ref, kn_ref, vn_ref, kt_ref, vt_ref, o_ref, *, ns, nh, w_buf):
    d = nh * HEAD_DIM
    ng = len(DIL_GROUPS)
    rows = ng * ns * nh
    contract_last = (((1,), (1,)), ((), ()))

    q = q_ref[...]
    head_of_lane = lax.broadcasted_iota(I32, (nh, d), 1) // HEAD_DIM
    head_of_row = lax.broadcasted_iota(I32, (nh, d), 0)
    pieces = []
    for g in range(ng):
        for qi in range(ns):
            qrow = jnp.broadcast_to(q[qi:qi + 1, g * d:(g + 1) * d], (nh, d))
            pieces.append(jnp.where(head_of_lane == head_of_row, qrow, 0.0))
    qf = jnp.concatenate(pieces, axis=0)
    qb = (qf * ATTN_SCALE).astype(BF16)

    kn, vn = kn_ref[...], vn_ref[...]
    s_new = [jnp.sum(qf * kn[j:j + 1, :], axis=1, keepdims=True) * ATTN_SCALE for j in range(ns)]

    r1 = lax.broadcasted_iota(I32, (rows, 1), 0)
    g1, q1 = r1 // (ns * nh), (r1 // nh) % ns
    win_r = jnp.zeros((rows, 1), I32)
    low_bits_r = jnp.zeros((rows, 1), I32)
    for g, (win, dil) in enumerate(DIL_GROUPS):
        win_r = jnp.where(g1 == g, win, win_r)
        low_bits_r = jnp.where(g1 == g, dil - 1, low_bits_r)

    s = jnp.dot(qb, kt_ref[...].astype(BF16), preferred_element_type=F32)
    dist = (w_buf + q1) - lax.broadcasted_iota(I32, (rows, w_buf), 1)
    s = jnp.where(((dist & low_bits_r) == 0) & (dist <= win_r), s, NEG_INF)
    s_new = [jnp.where((q1 >= j) & (((q1 - j) & low_bits_r) == 0) & (q1 - j <= win_r), s_new[j], NEG_INF)
             for j in range(ns)]
    m = functools.reduce(jnp.maximum, [jnp.max(s, axis=1, keepdims=True)] + s_new)
    p = jnp.exp(s - m)
    l = jnp.sum(p, axis=1, keepdims=True)
    acc = lax.dot_general(p.astype(BF16), vt_ref[...].astype(BF16), contract_last, preferred_element_type=F32)
    for j in range(ns):
        w = jnp.exp(s_new[j] - m)
        l = l + w
        acc = acc + w * vn[j:j + 1, :]
    o_all = acc / l
    lse_all = jnp.broadcast_to(m + jnp.log(l), o_all.shape)

    own = (lax.broadcasted_iota(I32, o_all.shape, 1) // HEAD_DIM) == (lax.broadcasted_iota(I32, o_all.shape, 0) % nh)
    o_gq = jnp.sum(jnp.where(own, o_all, 0.0).reshape(ng * ns, nh, d), axis=1)
    lse_gq = jnp.sum(jnp.where(own, lse_all, 0.0).reshape(ng * ns, nh, d), axis=1)
    o_ref[...] = _merge_groups([o_gq[g * ns:(g + 1) * ns] for g in range(ng)],
                               [lse_gq[g * ns:(g + 1) * ns] for g in range(ng)])


def _dilated_sample(q, k_new, v_new, cache_k, cache_v):
    nb, ns, d = k_new.shape
    _, w_buf, nh, _ = cache_k.shape
    ng = len(DIL_GROUPS)
    assert all(dil & (dil - 1) == 0 for _, dil in DIL_GROUPS)
    assert d == nh * HEAD_DIM and nh % 8 == 0 and w_buf % LANES == 0 and q.shape == (nb, ns, ng * d)
    transposed = lambda c: jnp.transpose(c, (0, 2, 3, 1)).reshape(nb, d, w_buf)
    new_spec = pl.BlockSpec((None, ns, d), lambda b: (b, 0, 0))
    buf_spec = pl.BlockSpec((None, d, w_buf), lambda b: (b, 0, 0))
    return pl.pallas_call(
        functools.partial(_dilated_sample_kernel, ns=ns, nh=nh, w_buf=w_buf),
        grid=(nb,),
        in_specs=[pl.BlockSpec((None, ns, ng * d), lambda b: (b, 0, 0)), new_spec, new_spec, buf_spec, buf_spec],
        out_specs=new_spec,
        out_shape=jax.ShapeDtypeStruct((nb, ns, d), F32),
        compiler_params=_cparams("parallel"),
        name="dilated_sample",
    )(q, k_new, v_new, transposed(cache_k), transposed(cache_v))


def _router_kernel(x_ref, g_ref, wr_ref, h_ref, route_ref):
    h = _rmsnorm_rows(x_ref[...], g_ref[...])
    h_ref[...] = h
    logits = jnp.dot(h, wr_ref[...], precision=HIGHEST, preferred_element_type=F32)
    lane = lax.broadcasted_iota(I32, logits.shape, 1)
    lane_f = lane.astype(F32)
    l1 = jnp.where(lane < N_EXPERTS, logits, NEG_INF)
    m1 = jnp.max(l1, axis=1, keepdims=True)
    i1 = jnp.min(jnp.where(l1 == m1, lane_f, float(LANES)), axis=1, keepdims=True)
    l2 = jnp.where(lane_f == i1, NEG_INF, l1)
    m2 = jnp.max(l2, axis=1, keepdims=True)
    i2 = jnp.min(jnp.where(l2 == m2, lane_f, float(LANES)), axis=1, keepdims=True)
    e = jnp.exp(m2 - m1)
    den = 1.0 + e
    packed = jnp.where(lane == 0, i1, jnp.where(lane == 1, i2,
             jnp.where(lane == 2, 1.0 / den, jnp.where(lane == 3, e / den, 0.0))))
    route_ref[...] = packed[:, :route_ref.shape[1]]


ROUTE_COLS = 8


def _router(x, g, w_router_padded):
    n, d = x.shape
    tm = min(TM_ROUTE, n)
    assert n % tm == 0
    return pl.pallas_call(
        _router_kernel,
        grid=(n // tm,),
        in_specs=[pl.BlockSpec((tm, d), lambda i: (i, 0)),
                  pl.BlockSpec((1, d), lambda i: (0, 0)),
                  pl.BlockSpec((d, LANES), lambda i: (0, 0))],
        out_specs=[pl.BlockSpec((tm, d), lambda i: (i, 0)),
                   pl.BlockSpec((tm, ROUTE_COLS), lambda i: (i, 0))],
        out_shape=[jax.ShapeDtypeStruct((n, d), F32), jax.ShapeDtypeStruct((n, ROUTE_COLS), F32)],
        compiler_params=_cparams("parallel"),
        name="moe_router",
    )(x, g.reshape(1, d), w_router_padded)


def _dispatch_kernel(dest_ref, h_ref, slots_in, slots_out, sem, *, tm):
    del slots_in

    def row_copy(r, k):
        return pltpu.make_async_copy(h_ref.at[pl.ds(r, 1)],
                                     slots_out.at[pl.ds(dest_ref[0, TOP_K * r + k], 1)], sem)

    def issue(r, c):
        for k in range(TOP_K):
            row_copy(r, k).start()
        return c

    def drain(r, c):
        for k in range(TOP_K):
            row_copy(r, k).wait()
        return c

    lax.fori_loop(0, tm, issue, 0, unroll=8)
    lax.fori_loop(0, tm, drain, 0, unroll=8)


def _dispatch(h, dest, slots):
    n, d = h.shape
    tm = min(TM_ROUTE, n)
    assert n % tm == 0
    return pl.pallas_call(
        functools.partial(_dispatch_kernel, tm=tm),
        grid=(n // tm,),
        in_specs=[pl.BlockSpec((None, 1, TOP_K * tm), lambda i: (i, 0, 0), memory_space=pltpu.SMEM),
                  pl.BlockSpec((tm, d), lambda i: (i, 0)),
                  pl.BlockSpec(memory_space=pl.ANY)],
        out_specs=pl.BlockSpec(memory_space=pl.ANY),
        out_shape=jax.ShapeDtypeStruct(slots.shape, slots.dtype),
        scratch_shapes=[pltpu.SemaphoreType.DMA(())],
        input_output_aliases={2: 0},
        compiler_params=_cparams("arbitrary"),
        name="moe_dispatch",
    )(dest.reshape(n // tm, 1, TOP_K * tm), h, slots)


def _moe_gu_kernel(exp_ref, valid_ref, first_ref, x_ref, wg_ref, wu_ref, o_ref, wgb_ref, wub_ref):
    del exp_ref
    t = pl.program_id(1)

    @pl.when(first_ref[t] == 1)
    def _():
        wgb_ref[...] = wg_ref[...].astype(BF16)
        wub_ref[...] = wu_ref[...].astype(BF16)

    @pl.when(valid_ref[t] == 1)
    def _():
        x = x_ref[...].astype(BF16)
        gate = jnp.dot(x, wgb_ref[...], preferred_element_type=F32)
        up = jnp.dot(x, wub_ref[...], preferred_element_type=F32)
        o_ref[...] = ((gate * jax.nn.sigmoid(gate)) * up).astype(BF16)

    @pl.when(valid_ref[t] == 0)
    def _():
        o_ref[...] = jnp.zeros(o_ref.shape, o_ref.dtype)


def _moe_gu(slots, w_gu, tile_exp, tile_valid, tile_first):
    r, d = slots.shape
    f = w_gu.shape[2] // 2
    tm, tn = TM_MOE, TN_MOE_GU
    assert r % tm == 0 and f % tn == 0
    nj = f // tn
    return pl.pallas_call(
        _moe_gu_kernel,
        grid_spec=pltpu.PrefetchScalarGridSpec(
            num_scalar_prefetch=3,
            grid=(nj, r // tm),
            in_specs=[pl.BlockSpec((tm, d), lambda j, t, ex, va, fi: (t, 0)),
                      pl.BlockSpec((None, d, tn), lambda j, t, ex, va, fi: (ex[t], 0, j)),
                      pl.BlockSpec((None, d, tn), lambda j, t, ex, va, fi: (ex[t], 0, nj + j))],
            out_specs=pl.BlockSpec((tm, tn), lambda j, t, ex, va, fi: (t, j)),
            scratch_shapes=[pltpu.VMEM((d, tn), BF16), pltpu.VMEM((d, tn), BF16)]),
        out_shape=jax.ShapeDtypeStruct((r, f), BF16),
        compiler_params=_cparams("arbitrary", "arbitrary"),
        name="moe_gate_up",
    )(tile_exp, tile_valid, tile_first, slots, w_gu, w_gu)


def _moe_down_kernel(exp_ref, valid_ref, first_ref, a_ref, wd_ref, o_ref, wdb_ref):
    del exp_ref
    t = pl.program_id(1)

    @pl.when(first_ref[t] == 1)
    def _():
        wdb_ref[...] = wd_ref[...].astype(BF16)

    @pl.when(valid_ref[t] == 1)
    def _():
        o_ref[...] = jnp.dot(a_ref[...], wdb_ref[...], preferred_element_type=F32)

    @pl.when(valid_ref[t] == 0)
    def _():
        o_ref[...] = jnp.zeros(o_ref.shape, o_ref.dtype)


def _moe_down(act, w_down, tile_exp, tile_valid, tile_first):
    r, f = act.shape
    d = w_down.shape[2]
    tm, tn = TM_MOE, TN_MOE_DOWN
    assert r % tm == 0 and d % tn == 0
    return pl.pallas_call(
        _moe_down_kernel,
        grid_spec=pltpu.PrefetchScalarGridSpec(
            num_scalar_prefetch=3,
            grid=(d // tn, r // tm),
            in_specs=[pl.BlockSpec((tm, f), lambda j, t, ex, va, fi: (t, 0)),
                      pl.BlockSpec((None, f, tn), lambda j, t, ex, va, fi: (ex[t], 0, j))],
            out_specs=pl.BlockSpec((tm, tn), lambda j, t, ex, va, fi: (t, j)),
            scratch_shapes=[pltpu.VMEM((f, tn), BF16)]),
        out_shape=jax.ShapeDtypeStruct((r, d), F32),
        compiler_params=_cparams("arbitrary", "arbitrary"),
        name="moe_down",
    )(tile_exp, tile_valid, tile_first, act, w_down)


def _combine_kernel(dest_ref, x_ref, route_ref, rows_hbm, y_ref, buf_ref, sem, *, tm):
    def row_copy(r, k):
        return pltpu.make_async_copy(rows_hbm.at[pl.ds(dest_ref[0, TOP_K * r + k], 1)],
                                     buf_ref.at[k, pl.ds(r, 1)], sem)

    def issue(r, c):
        for k in range(TOP_K):
            row_copy(r, k).start()
        return c

    def drain(r, c):
        for k in range(TOP_K):
            row_copy(r, k).wait()
        return c

    lax.fori_loop(0, tm, issue, 0, unroll=8)
    lax.fori_loop(0, tm, drain, 0, unroll=8)
    route = route_ref[...]
    y_ref[...] = x_ref[...] + (route[:, 2:3] * buf_ref[0] + route[:, 3:4] * buf_ref[1])


def _combine(x, route, dest, rows):
    n, d = x.shape
    tm = min(TM_ROUTE, n)
    assert n % tm == 0
    return pl.pallas_call(
        functools.partial(_combine_kernel, tm=tm),
        grid=(n // tm,),
        in_specs=[pl.BlockSpec((None, 1, TOP_K * tm), lambda i: (i, 0, 0), memory_space=pltpu.SMEM),
                  pl.BlockSpec((tm, d), lambda i: (i, 0)),
                  pl.BlockSpec((tm, ROUTE_COLS), lambda i: (i, 0)),
                  pl.BlockSpec(memory_space=pl.ANY)],
        out_specs=pl.BlockSpec((tm, d), lambda i: (i, 0)),
        out_shape=jax.ShapeDtypeStruct((n, d), F32),
        scratch_shapes=[pltpu.VMEM((TOP_K, tm, d), F32), pltpu.SemaphoreType.DMA(())],
        compiler_params=_cparams("arbitrary"),
        name="moe_combine",
    )(dest.reshape(n // tm, 1, TOP_K * tm), x, route, rows)


def _moe_ffn(xs, g, w_router, w_gu, w_down):
    d = xs[0].shape[1]
    wr = jnp.zeros((d, LANES), F32).at[:, :N_EXPERTS].set(w_router)
    routed = [_router(x, g, wr) for x in xs]
    e_flat = jnp.concatenate([route[:, :TOP_K] for _, route in routed], axis=0).astype(I32).reshape(-1)
    a = e_flat.shape[0]
    onehot = (e_flat[:, None] == jnp.arange(N_EXPERTS, dtype=I32)[None, :]).astype(I32)
    csum = jnp.cumsum(onehot, axis=0)
    rank = jnp.sum(csum * onehot, axis=1) - 1
    counts = csum[-1]
    padded = (counts + TM_MOE - 1) // TM_MOE * TM_MOE
    pends = jnp.cumsum(padded)
    dest = (jnp.sum((pends - padded)[None, :] * onehot, axis=1) + rank).astype(I32)
    n_tiles = -(-a // TM_MOE) + N_EXPERTS
    tile_start = jnp.arange(n_tiles, dtype=I32) * TM_MOE
    tile_exp = jnp.minimum(jnp.sum(tile_start[:, None] >= pends[None, :], axis=1), N_EXPERTS - 1).astype(I32)
    tile_valid = (tile_start < pends[-1]).astype(I32)
    tile_first = jnp.concatenate([jnp.ones((1,), I32), (tile_exp[1:] != tile_exp[:-1]).astype(I32)])

    slots = jnp.zeros((n_tiles * TM_MOE, d), F32)
    dests, off = [], 0
    for h, _ in routed:
        dests.append(dest[off:off + TOP_K * h.shape[0]])
        slots = _dispatch(h, dests[-1], slots)
        off += TOP_K * h.shape[0]
    act = _moe_gu(slots, w_gu, tile_exp, tile_valid, tile_first)
    rows = _moe_down(act, w_down, tile_exp, tile_valid, tile_first)
    return [_combine(x, route, dst, rows) for x, (_, route), dst in zip(xs, routed, dests)]


def kernel(x_prompt, x_sample, cache_a_k, cache_a_v, cache_b_k, cache_b_v, page_table, norm_attn_a, w_qkv_a,
           qnorm_a, knorm_a, w_o_a, norm_kv_b, w_kv_b, knorm_b, norm_attn_b, w_q_b, qnorm_b, w_o_b, norm_ffn,
           w_gu_dense, w_down_dense, w_router, w_gu_moe, w_down_moe):
    b, s, d = x_prompt.shape
    nb, ns, _ = x_sample.shape
    depth, n_a = norm_ffn.shape[0], w_qkv_a.shape[0]
    n_phys, page = cache_a_k.shape[1], cache_a_k.shape[2]
    past = page_table.shape[1] * page
    heads_a = d // HEAD_DIM
    d_b = w_kv_b.shape[1] // 2
    heads_b = d_b // HEAD_DIM
    n_groups = len(DIL_GROUPS)
    w_buf = cache_b_k.shape[1]
    w_max = max(w for w, _ in DIL_GROUPS)

    xp = x_prompt.reshape(b * s, d)
    xs = x_sample.reshape(nb * ns, d)
    tabs_p = _rope_tables(jnp.tile(jnp.arange(s), b))
    tabs_s = _rope_tables(jnp.tile(past + jnp.arange(ns), nb))
    ones_d = jnp.ones((d,), F32)
    ak_p, av_p, ak_s, av_s = [], [], [], []
    for l in range(depth):
        if l < n_a:
            gains = jnp.concatenate([jnp.tile(qnorm_a[l], heads_a), jnp.tile(knorm_a[l], heads_a), ones_d])
            qp, kp, kp_t, vp, vp_t = _norm_proj(xp, norm_attn_a[l], w_qkv_a[l], gains, tabs_p,
                                                ((d, True, False), (d, True, True), (d, False, True)), seq=s)
            qs, ks, vs = _norm_proj(xs, norm_attn_a[l], w_qkv_a[l], gains, tabs_s,
                                    ((d, True, False), (d, True, False), (d, False, False)))
            op = _moba_prompt(qp.reshape(b, s, d), kp.reshape(b, s, d), vp.reshape(b, s, d))
            os_ = _moba_sample(qs.reshape(nb, ns, d), ks.reshape(nb, ns, d), vs.reshape(nb, ns, d),
                               cache_a_k[l], cache_a_v[l], page_table)
            xp = _matmul_residual(op.reshape(b * s, d), w_o_a[l], xp)
            xs = _matmul_residual(os_.reshape(nb * ns, d), w_o_a[l], xs)
            ak_p.append(kp_t)
            av_p.append(vp_t)
            ak_s.append(ks)
            av_s.append(vs)
        else:
            j = l - n_a
            gains = jnp.tile(qnorm_b[j], n_groups * heads_b)
            parts = ((n_groups * d_b, True, False),)
            (qp,) = _norm_proj(xp, norm_attn_b[j], w_q_b[j], gains, tabs_p, parts)
            (qs,) = _norm_proj(xs, norm_attn_b[j], w_q_b[j], gains, tabs_s, parts)
            op = _dilated_prompt(qp.reshape(b, s, n_groups * d_b), kb_p.reshape(b, s, d_b), vb_p.reshape(b, s, d_b))
            os_ = _dilated_sample(qs.reshape(nb, ns, n_groups * d_b), kb_s.reshape(nb, ns, d_b),
                                  vb_s.reshape(nb, ns, d_b), cache_b_k, cache_b_v)
            xp = _matmul_residual(op.reshape(b * s, d_b), w_o_b[j], xp)
            xs = _matmul_residual(os_.reshape(nb * ns, d_b), w_o_b[j], xs)
        if l % 2 == 0:
            xp = _dense_ffn(xp, norm_ffn[l], w_gu_dense[l // 2], w_down_dense[l // 2])
            xs = _dense_ffn(xs, norm_ffn[l], w_gu_dense[l // 2], w_down_dense[l // 2])
        else:
            xp, xs = _moe_ffn([xp, xs], norm_ffn[l], w_router[l // 2], w_gu_moe[l // 2], w_down_moe[l // 2])
        if l == n_a - 1:
            gains = jnp.concatenate([jnp.tile(knorm_b, heads_b), jnp.ones((d_b,), F32)])
            kb_p, kb_pt, vb_p, vb_pt = _norm_proj(xp, norm_kv_b, w_kv_b, gains, tabs_p,
                                                  ((d_b, True, True), (d_b, False, True)), seq=s)
            kb_s, vb_s = _norm_proj(xs, norm_kv_b, w_kv_b, gains, tabs_s, ((d_b, True, False), (d_b, False, False)))

    keep = min(w_max, s)
    heads = lambda t, n, h: t.reshape(n_a, *n, h, HEAD_DIM)
    seq_major = lambda t, h: jnp.moveaxis(t.reshape(*t.shape[:-2], h, HEAD_DIM, t.shape[-1]), -1, -3)
    return (xp.reshape(b, s, d), xs.reshape(nb, ns, d),
            seq_major(jnp.stack(ak_p), heads_a), seq_major(jnp.stack(av_p), heads_a),
            seq_major(kb_pt, heads_b)[:, s - keep:], seq_major(vb_pt, heads_b)[:, s - keep:],
            heads(jnp.stack(ak_s), (nb, ns), heads_a), heads(jnp.stack(av_s), (nb, ns), heads_a),
            kb_s.reshape(nb, ns, heads_b, HEAD_DIM), vb_s.reshape(nb, ns, heads_b, HEAD_DIM))
```
